```python
import jax, jax.numpy as jnp
from jax import lax
import numpy as np

D_MODEL = 1024
BATCH = 8
SEQ = 8192
DEPTH = 2

RWKV_HEADS = 16
RWKV_HEAD_DIM = 64
RWKV_WIDTH = RWKV_HEADS * RWKV_HEAD_DIM
DECAY_LORA = 64
ICLR_LORA = 64
VRES_LORA = 32
RWKV_GN_EPS = 64e-5
SSM_WIDTH = 2 * D_MODEL
SSM_HEAD_DIM = 64
SSM_HEADS = SSM_WIDTH // SSM_HEAD_DIM
SSM_GROUPS = 4
SSM_STATE = 128
CONV_WIDTH = 4
SSD_CHUNK = 128
CONV_CH = SSM_WIDTH + 2 * SSM_GROUPS * SSM_STATE
RMS_EPS = 1e-5

RWKV_COLS = 4 * RWKV_WIDTH + DECAY_LORA + ICLR_LORA
SSM_COLS = SSM_WIDTH + CONV_CH + SSM_HEADS
GATE_COLS = 2 * D_MODEL
PROJ_COLS = RWKV_COLS + SSM_COLS + GATE_COLS
RWKV_SPLITS = [RWKV_WIDTH, 2 * RWKV_WIDTH, 3 * RWKV_WIDTH, 3 * RWKV_WIDTH + DECAY_LORA, 3 * RWKV_WIDTH + DECAY_LORA + ICLR_LORA]
SSM_SPLITS = [SSM_WIDTH, SSM_WIDTH + CONV_CH]

kernel_name = 'hybrid_rwkv7_mamba2_gated_parallel'


def rmsnorm(x, w):
    xf = x.astype(jnp.float32)
    y = xf * lax.rsqrt(jnp.mean(xf * xf, axis=-1, keepdims=True) + RMS_EPS)
    return (y * w.astype(jnp.float32)).astype(x.dtype)


def token_shift_lerp(u, mu):
    u_prev = jnp.pad(u, ((0, 0), (1, 0), (0, 0)))[:, :-1]
    return u + (u_prev - u) * mu


def rwkv7_recurrence(r, decay, k, v, a_vec, b_vec):
    bsz, _, h, n = r.shape
    xs = tuple(jnp.moveaxis(t, 1, 0) for t in (r, decay, k, v, a_vec, b_vec))

    def step(state, inp):
        r_t, w_t, k_t, v_t, a_t, b_t = inp
        sa = jnp.einsum('bhvk,bhk->bhv', state, a_t)
        state = (state * w_t[:, :, None, :] + sa[..., :, None] * b_t[..., None, :]
                 + v_t[..., :, None] * k_t[..., None, :])
        y = jnp.einsum('bhvk,bhk->bhv', state, r_t)
        return state, y

    s0 = jnp.zeros((bsz, h, n, n), jnp.float32)
    _, ys = lax.scan(step, s0, xs)
    return jnp.moveaxis(ys, 0, 1)


def rwkv7_branch(rw, v_first, vres, w0, w2, a0, a2, k_k, k_a, r_k, gn_w, gn_b):
    f32 = jnp.float32
    bsz, s = rw.shape[0], rw.shape[1]
    r, k, v, w_lo, a_lo, z = jnp.split(rw.astype(f32), RWKV_SPLITS, axis=-1)
    w_log = -jax.nn.softplus(-(w0 + jnp.tanh(w_lo) @ w2)) - 0.5
    decay = jnp.exp(-jnp.exp(w_log))
    a = jax.nn.sigmoid(a0 + a_lo @ a2)
    hs = lambda t: t.reshape(bsz, s, RWKV_HEADS, RWKV_HEAD_DIM)
    kk = hs(k * k_k)
    kk = kk / jnp.maximum(jnp.sqrt(jnp.sum(kk * kk, axis=-1, keepdims=True)), 1e-12)
    k = k * (1.0 + (a - 1.0) * k_a)
    if vres is None:
        v_first = v
    else:
        vres_lo, v0, v2 = vres
        v = v + (v_first - v) * jax.nn.sigmoid(v0 + vres_lo.astype(f32) @ v2)
    rh, kh, vh = hs(r), hs(k), hs(v)
    y = rwkv7_recurrence(rh, hs(decay), kh, vh, -kk, kk * hs(a))
    mean = jnp.mean(y, axis=-1, keepdims=True)
    var = jnp.mean(jnp.square(y - mean), axis=-1, keepdims=True)
    y = ((y - mean) * lax.rsqrt(var + RWKV_GN_EPS)).reshape(bsz, s, RWKV_WIDTH) * gn_w + gn_b
    bonus = jnp.sum(rh * kh * r_k, axis=-1, keepdims=True) * vh
    y = (y + bonus.reshape(bsz, s, RWKV_WIDTH)) * jax.nn.silu(z)
    return y.astype(rw.dtype), v_first


def causal_depthwise_conv(u, w, b):
    c = u.shape[-1]
    out = lax.conv_general_dilated(u, w[:, None, :].astype(u.dtype), window_strides=(1,),
                                   padding=[(CONV_WIDTH - 1, 0)],
                                   dimension_numbers=('NWC', 'WIO', 'NWC'),
                                   feature_group_count=c)
    return out + b


def ssd_chunked(xh, dt, a_neg, bm, cm):
    bsz, s, h, p = xh.shape
    g, n = bm.shape[2], bm.shape[3]
    hg = h // g
    nc = s // SSD_CHUNK
    chunks = lambda t: jnp.moveaxis(t.reshape(bsz, nc, SSD_CHUNK, *t.shape[2:]), 1, 0)
    xdt = chunks((xh * dt[..., None]).reshape(bsz, s, g, hg, p))
    da = chunks((dt * a_neg).reshape(bsz, s, g, hg))
    causal = jnp.tril(jnp.ones((SSD_CHUNK, SSD_CHUNK), bool))[None, :, :, None, None]

    def step(state, inp):
        x_c, da_c, b_c, c_c = inp
        cum = jnp.cumsum(da_c, axis=1)
        seg = cum[:, :, None] - cum[:, None, :]
        decay_mat = jnp.exp(jnp.where(causal, seg, -jnp.inf))
        cb = jnp.einsum('blgn,bsgn->blsg', c_c, b_c)
        y_intra = jnp.einsum('blsg,blsgh,bsghp->blghp', cb, decay_mat, x_c)
        y_inter = jnp.einsum('blgn,bghpn,blgh->blghp', c_c, state, jnp.exp(cum))
        tail = jnp.exp(cum[:, -1:] - cum)
        state = (state * jnp.exp(cum[:, -1])[..., None, None]
                 + jnp.einsum('bsgn,bsgh,bsghp->bghpn', b_c, tail, x_c))
        return state, y_intra + y_inter

    s0 = jnp.zeros((bsz, g, hg, p, n), jnp.float32)
    _, ys = lax.scan(step, s0, (xdt, da, chunks(bm), chunks(cm)))
    return jnp.moveaxis(ys, 0, 1).reshape(bsz, s, h, p)


def mamba2_branch(sm, conv_w, conv_b, dt_bias, a_log, d_skip, norm_w):
    f32 = jnp.float32
    bsz, s = sm.shape[0], sm.shape[1]
    z, xbc, dt_raw = jnp.split(sm.astype(f32), SSM_SPLITS, axis=-1)
    xbc = jax.nn.silu(causal_depthwise_conv(xbc, conv_w.astype(f32), conv_b))
    xs, bm, cm = jnp.split(xbc, [SSM_WIDTH, SSM_WIDTH + SSM_GROUPS * SSM_STATE], axis=-1)
    xh = xs.reshape(bsz, s, SSM_HEADS, SSM_HEAD_DIM)
    bm = bm.reshape(bsz, s, SSM_GROUPS, SSM_STATE)
    cm = cm.reshape(bsz, s, SSM_GROUPS, SSM_STATE)
    dt = jax.nn.softplus(dt_raw + dt_bias)
    a_neg = -jnp.exp(a_log.astype(f32))
    y = ssd_chunked(xh, dt, a_neg, bm, cm) + d_skip[:, None] * xh
    y = y.reshape(bsz, s, SSM_WIDTH) * jax.nn.silu(z)
    yg = y.reshape(bsz, s, SSM_GROUPS, SSM_WIDTH // SSM_GROUPS)
    yg = yg * lax.rsqrt(jnp.mean(yg * yg, axis=-1, keepdims=True) + RMS_EPS)
    return (yg.reshape(bsz, s, SSM_WIDTH) * norm_w).astype(sm.dtype)


def setup_inputs(seed: int = 0) -> dict:
    key = jax.random.key(seed)
    ks = jax.random.split(key, 32)
    nrm = lambda k, shape, scale: jax.random.normal(k, shape, jnp.float32) * scale
    L1 = DEPTH - 1
    dt0 = jnp.exp(jax.random.uniform(ks[20], (DEPTH, SSM_HEADS), jnp.float32,
                                     np.log(1e-3), np.log(1e-1)))
    return {
        'x': nrm(ks[0], (BATCH, SEQ, D_MODEL), 1.0),
        'norm_w': 1.0 + nrm(ks[1], (DEPTH, D_MODEL), 0.02),
        'w_in': nrm(ks[2], (DEPTH, D_MODEL, PROJ_COLS), D_MODEL ** -0.5),
        'w_in_vres': nrm(ks[3], (L1, D_MODEL, VRES_LORA), D_MODEL ** -0.5),
        'mu_rwkv': jax.random.uniform(ks[4], (DEPTH, RWKV_COLS), jnp.float32),
        'mu_vres': jax.random.uniform(ks[5], (L1, VRES_LORA), jnp.float32),
        'decay_w0': jax.random.uniform(ks[6], (DEPTH, RWKV_WIDTH), jnp.float32, -6.0, -1.0),
        'decay_w2': nrm(ks[7], (DEPTH, DECAY_LORA, RWKV_WIDTH), 0.1),
        'iclr_a0': nrm(ks[8], (DEPTH, RWKV_WIDTH), 0.1),
        'iclr_a2': nrm(ks[9], (DEPTH, ICLR_LORA, RWKV_WIDTH), 0.1),
        'vres_v0': 1.0 + nrm(ks[10], (L1, RWKV_WIDTH), 0.1),
        'vres_v2': nrm(ks[11], (L1, VRES_LORA, RWKV_WIDTH), 0.1),
        'k_k': 0.85 + nrm(ks[12], (DEPTH, RWKV_WIDTH), 0.02),
        'k_a': 1.0 + nrm(ks[13], (DEPTH, RWKV_WIDTH), 0.02),
        'r_k': nrm(ks[14], (DEPTH, RWKV_HEADS, RWKV_HEAD_DIM), 0.1),
        'gn_w': 1.0 + nrm(ks[15], (DEPTH, RWKV_WIDTH), 0.02),
        'gn_b': nrm(ks[16], (DEPTH, RWKV_WIDTH), 0.02),
        'w_out_rwkv': nrm(ks[17], (DEPTH, RWKV_WIDTH, D_MODEL), RWKV_WIDTH ** -0.5),
        'conv_w': nrm(ks[18], (DEPTH, CONV_WIDTH, CONV_CH), CONV_WIDTH ** -0.5),
        'conv_b': nrm(ks[19], (DEPTH, CONV_CH), 0.02),
        'dt_bias': dt0 + jnp.log(-jnp.expm1(-dt0)),
        'a_log': jnp.log(jax.random.uniform(ks[21], (DEPTH, SSM_HEADS), jnp.float32, 1.0, 16.0)),
        'd_skip': 1.0 + nrm(ks[22], (DEPTH, SSM_HEADS), 0.02),
        'ssm_norm_w': 1.0 + nrm(ks[23], (DEPTH, SSM_WIDTH), 0.02),
        'w_out_ssm': nrm(ks[24], (DEPTH, SSM_WIDTH, D_MODEL), SSM_WIDTH ** -0.5),
        'w_out': nrm(ks[25], (DEPTH, D_MODEL, D_MODEL), D_MODEL ** -0.5),
        'final_norm_w': 1.0 + nrm(ks[26], (D_MODEL,), 0.02),
    }


def reference(x, norm_w, w_in, w_in_vres, mu_rwkv, mu_vres, decay_w0, decay_w2, iclr_a0, iclr_a2,
              vres_v0, vres_v2, k_k, k_a, r_k, gn_w, gn_b, w_out_rwkv, conv_w, conv_b, dt_bias,
              a_log, d_skip, ssm_norm_w, w_out_ssm, w_out, final_norm_w):
    v_first = None
    for i in range(DEPTH):
        xn = rmsnorm(x, norm_w[i])
        w_cat = w_in[i] if i == 0 else jnp.concatenate([w_in[i], w_in_vres[i - 1]], axis=-1)
        proj = xn @ w_cat.astype(x.dtype)
        rw = token_shift_lerp(proj[..., :RWKV_COLS], mu_rwkv[i].astype(x.dtype))
        sm = proj[..., RWKV_COLS:RWKV_COLS + SSM_COLS]
        g_rwkv, g_ssm = jnp.split(proj[..., RWKV_COLS + SSM_COLS:PROJ_COLS], [D_MODEL], axis=-1)
        vres = None
        if i > 0:
            vres_lo = token_shift_lerp(proj[..., PROJ_COLS:], mu_vres[i - 1].astype(x.dtype))
            vres = (vres_lo, vres_v0[i - 1], vres_v2[i - 1])
        y_rwkv, v_first = rwkv7_branch(rw, v_first, vres, decay_w0[i], decay_w2[i], iclr_a0[i],
                                       iclr_a2[i], k_k[i], k_a[i], r_k[i], gn_w[i], gn_b[i])
        y_ssm = mamba2_branch(sm, conv_w[i], conv_b[i], dt_bias[i], a_log[i], d_skip[i], ssm_norm_w[i])
        p_rwkv = y_rwkv @ w_out_rwkv[i].astype(x.dtype)
        p_ssm = y_ssm @ w_out_ssm[i].astype(x.dtype)
        h = jax.nn.sigmoid(g_rwkv) * p_rwkv + jax.nn.sigmoid(g_ssm) * p_ssm
        x = x + h @ w_out[i].astype(x.dtype)
    return rmsnorm(x, final_norm_w)
```

```python
import functools

import jax
import jax.numpy as jnp
from jax import lax
from jax.experimental import pallas as pl
from jax.experimental.pallas import tpu as pltpu

F32 = jnp.float32
BF16 = jnp.bfloat16

D_MODEL = 1024
RWKV_HEADS = 16
RWKV_HEAD_DIM = 64
RWKV_WIDTH = RWKV_HEADS * RWKV_HEAD_DIM
DECAY_LORA = 64
ICLR_LORA = 64
VRES_LORA = 32
RWKV_GN_EPS = 64e-5
SSM_WIDTH = 2 * D_MODEL
SSM_HEAD_DIM = 64
SSM_HEADS = SSM_WIDTH // SSM_HEAD_DIM
SSM_GROUPS = 4
SSM_STATE = 128
CONV_WIDTH = 4
SSD_CHUNK = 128
RMS_EPS = 1e-5

LANES = 128
SUBLANES = 8
RWKV_CHUNK = 64
HEAD_GROUP = 4
GROUP_W = HEAD_GROUP * RWKV_HEAD_DIM
SSM_GROUP_W = SSM_WIDTH // SSM_GROUPS
SSM_GROUP_HEADS = SSM_HEADS // SSM_GROUPS

OFF_SSM_Z = 0
OFF_G_RWKV = 2048
OFF_G_SSM = 3072
OFF_R = 4096
OFF_K = 5120
OFF_V = 6144
OFF_Z = 7168
OFF_XS = 8192
OFF_B = 10240
OFF_C = 10752
OFF_LORA = 11264
OFF_VRES = 11392
OFF_DT = 11520
PROJ_W = 12288

VMEM_LIMIT = 56 * 1024 * 1024


def _sigmoid(x):
    return 1.0 / (1.0 + jnp.exp(-x))


def _softplus(x):
    return jnp.maximum(x, 0.0) + jnp.log(1.0 + jnp.exp(-jnp.abs(x)))


def _dot(a, b):
    return jnp.dot(a.astype(BF16), b.astype(BF16), preferred_element_type=F32)


def _dot_nt(a, b):
    return lax.dot_general(a.astype(BF16), b.astype(BF16), (((1,), (1,)), ((), ())),
                           preferred_element_type=F32)


def _dot_tn(a, b):
    return lax.dot_general(a.astype(BF16), b.astype(BF16), (((0,), (0,)), ((), ())),
                           preferred_element_type=F32)


def _split3(x):
    hi = x.astype(BF16)
    r1 = x - hi.astype(F32)
    mid = r1.astype(BF16)
    lo = (r1 - mid.astype(F32)).astype(BF16)
    return hi, mid, lo


def _dot3_right(x, w01):
    return sum(jnp.dot(p, w01, preferred_element_type=F32) for p in _split3(x))


def _dot3_left(w01, x):
    return sum(jnp.dot(w01, p, preferred_element_type=F32) for p in _split3(x))


def _iota(shape, axis):
    return lax.broadcasted_iota(jnp.int32, shape, axis)


def _prev_rows(u, carry, s):
    us = pltpu.roll(u, s, axis=0)
    cs = pltpu.roll(carry, s, axis=0)
    head = jnp.where(_iota(carry.shape, 0) < s, cs, us[0:SUBLANES])
    return jnp.concatenate([head, us[SUBLANES:]], axis=0)


def _inproj_kernel(x_ref, nw_ref, w_ref, o_ref, xn_ref):
    @pl.when(pl.program_id(1) == 0)
    def _():
        x = x_ref[...]
        ms = jnp.mean(x * x, axis=-1, keepdims=True)
        xn_ref[...] = (x * lax.rsqrt(ms + RMS_EPS) * nw_ref[...]).astype(BF16)

    o_ref[...] = jnp.dot(xn_ref[...], w_ref[...], preferred_element_type=F32).astype(o_ref.dtype)


def _inproj(x2, nw, w, tm, tn, out_dtype):
    t, d = x2.shape
    npad = w.shape[1]
    return pl.pallas_call(
        _inproj_kernel,
        grid=(t // tm, npad // tn),
        in_specs=[
            pl.BlockSpec((tm, d), lambda i, j: (i, 0)),
            pl.BlockSpec((1, d), lambda i, j: (0, 0)),
            pl.BlockSpec((d, tn), lambda i, j: (0, j)),
        ],
        out_specs=pl.BlockSpec((tm, tn), lambda i, j: (i, j)),
        out_shape=jax.ShapeDtypeStruct((t, npad), out_dtype),
        scratch_shapes=[pltpu.VMEM((tm, d), BF16)],
        compiler_params=pltpu.CompilerParams(
            dimension_semantics=("arbitrary", "arbitrary"), vmem_limit_bytes=VMEM_LIMIT),
        name="inproj",
    )(x2, nw, w)


def _block_diag(x, lane_head):
    return jnp.concatenate(
        [jnp.where(lane_head == h, x, 0.0) for h in range(HEAD_GROUP)], axis=0)


def _rwkv_chunk(s0, r, k, v, ld, av, bv):
    L = RWKV_CHUNK
    W = GROUP_W
    row = _iota((L, W), 0)
    lane = _iota((L, W), 1)
    lane_head = lane >> 6
    lane_in = lane & (RWKV_HEAD_DIM - 1)
    incl = lane_in <= row
    strict = lane_in < row
    eye = jnp.where(lane_in == row, 1.0, 0.0).astype(F32)
    tri = jnp.where(_iota((L, L), 1) <= _iota((L, L), 0), 1.0, 0.0).astype(BF16)

    cum = _dot3_left(tri, ld)
    cum_last = cum[L - 1:L, :]
    g = jnp.exp(cum)
    g_inv = jnp.exp(-cum)
    g_prev = jnp.exp(cum - ld)
    g_tail = jnp.exp(cum_last - cum)
    g_last = jnp.exp(cum_last)
    r_t = r * g
    a_t = av * g_prev
    b_t = bv * g_inv
    k_t = k * g_inv
    b_e = bv * g_tail
    k_e = k * g_tail

    bd = functools.partial(_block_diag, lane_head=lane_head)
    prod = _dot_nt(jnp.concatenate([a_t, r_t], axis=0),
                   jnp.concatenate([bd(b_t), bd(k_t)], axis=0))
    m_ab = jnp.where(strict, prod[:L, :W], 0.0)
    m_ak = jnp.where(strict, prod[:L, W:], 0.0)
    p_rb = jnp.where(incl, prod[L:, :W], 0.0)
    p_rk = jnp.where(incl, prod[L:, W:], 0.0)

    x = _dot(m_ab, bd(m_ab))
    t_inv = eye + m_ab
    for _ in range(4):
        q = _dot(jnp.concatenate([x, t_inv], axis=0), bd(x))
        x = q[:L]
        t_inv = t_inv + q[L:]
    t_inv = t_inv + _dot(t_inv, bd(x))

    bd_v = bd(v)
    mv = _dot(m_ak, bd_v)
    au = _dot(t_inv, jnp.concatenate([bd(a_t), bd(mv)], axis=1))
    a_hat = au[:, :W]
    u0 = au[:, W:]
    ry = _dot(p_rb, jnp.concatenate([bd(a_hat), bd(u0)], axis=1))
    r_hat = r_t + ry[:, :W]
    y0 = ry[:, W:] + _dot(p_rk, bd_v)

    row2 = _iota((W, W), 0)
    lane2 = _iota((W, W), 1)
    same_head = (row2 >> 6) == (lane2 >> 6)
    g_mat = jnp.where(same_head, _dot_tn(a_hat, b_e), 0.0) + jnp.where(row2 == lane2, g_last, 0.0)
    h_mat = jnp.where(same_head, _dot_tn(u0, b_e) + _dot_tn(v, k_e), 0.0)

    y = _dot_nt(r_hat, s0) + y0
    s1 = _dot(s0, g_mat) + h_mat
    return y, s1


def _rwkv_kernel(has_vres, n_chunks, *refs):
    it = iter(refs)
    r_ref, k_ref, v_ref, z_ref, lora_ref = (next(it) for _ in range(5))
    vres_ref = next(it) if has_vres else None
    vf_ref = next(it) if has_vres else None
    pv_ref, mul_ref, w2_ref, a2_ref = (next(it) for _ in range(4))
    v2_ref = next(it) if has_vres else None
    y_ref = next(it)
    vout_ref = None if has_vres else next(it)
    s_ref, cr, ck, cv, cz, cl, cvr = (next(it) for _ in range(7))
    q_r, q_k, q_v, q_ld, q_a, q_b, q_y = (next(it) for _ in range(7))

    @pl.when(pl.program_id(2) == 0)
    def _():
        s_ref[...] = jnp.zeros_like(s_ref)
        for c in (cr, ck, cv, cz, cl, cvr):
            c[...] = jnp.zeros_like(c)

    tl = r_ref.shape[0]
    pv = pv_ref[...]
    prm = lambda i: pv[i:i + 1, :]
    mu_r, mu_k, mu_v, mu_z, w0, a0, k_k, k_a, r_k, gn_w, gn_b, v0 = (prm(i) for i in range(12))

    def shifted(ref, carry, mu):
        u = ref[...].astype(F32)
        prev = _prev_rows(u, carry[...], 1)
        carry[...] = u[tl - SUBLANES:tl, :]
        return u + (prev - u) * mu

    r = shifted(r_ref, cr, mu_r)
    k = shifted(k_ref, ck, mu_k)
    v = shifted(v_ref, cv, mu_v)
    lora = shifted(lora_ref, cl, mul_ref[0:1, :])

    w_log = -_softplus(-(w0 + _dot(jnp.tanh(lora), w2_ref[...]))) - 0.5
    ld = -jnp.exp(w_log)
    a = _sigmoid(a0 + _dot(lora, a2_ref[...]))

    row = _iota((GROUP_W, GROUP_W), 0)
    lane = _iota((GROUP_W, GROUP_W), 1)
    seg_ones = jnp.where((row >> 6) == (lane >> 6), 1.0, 0.0).astype(BF16)
    head_sum = lambda t: _dot3_right(t, seg_ones)

    kk = k * k_k
    kk = kk / jnp.maximum(jnp.sqrt(head_sum(kk * kk)), 1e-12)
    k = k * (1.0 + (a - 1.0) * k_a)
    if has_vres:
        vres = shifted(vres_ref, cvr, mul_ref[1:2, :])
        v = v + (vf_ref[...] - v) * _sigmoid(v0 + _dot(vres, v2_ref[...]))
    else:
        vout_ref[...] = v

    q_r[...] = r
    q_k[...] = k
    q_v[...] = v
    q_ld[...] = ld
    q_a[...] = -kk
    q_b[...] = kk * a

    def chunk(c, carry):
        rows = pl.ds(pl.multiple_of(c * RWKV_CHUNK, RWKV_CHUNK), RWKV_CHUNK)
        y, s1 = _rwkv_chunk(s_ref[...], q_r[rows, :], q_k[rows, :], q_v[rows, :],
                            q_ld[rows, :], q_a[rows, :], q_b[rows, :])
        q_y[rows, :] = y
        s_ref[...] = s1
        return carry

    lax.fori_loop(0, n_chunks, chunk, 0)

    y = q_y[...]
    inv_n = 1.0 / RWKV_HEAD_DIM
    mean = head_sum(y) * inv_n
    yc = y - mean
    var = head_sum(yc * yc) * inv_n
    y = yc * lax.rsqrt(var + RWKV_GN_EPS) * gn_w + gn_b
    bonus = head_sum(r * k * r_k) * v
    z = shifted(z_ref, cz, mu_z)
    y_ref[...] = ((y + bonus) * (z * _sigmoid(z))).astype(y_ref.dtype)


def _rwkv(proj, v_first, pvec, mul, w2p, a2p, v2p, bsz, seq, tl):
    has_vres = v_first is not None
    nt = seq // tl
    t = bsz * seq
    gw = GROUP_W
    colblk = lambda off: (lambda b, g, i: (b * nt + i, off // gw + g))
    smallblk = lambda off: (lambda b, g, i: (b * nt + i, off // LANES))
    in_specs = [pl.BlockSpec((tl, gw), colblk(OFF_R)),
                pl.BlockSpec((tl, gw), colblk(OFF_K)),
                pl.BlockSpec((tl, gw), colblk(OFF_V)),
                pl.BlockSpec((tl, gw), colblk(OFF_Z)),
                pl.BlockSpec((tl, LANES), smallblk(OFF_LORA))]
    args = [proj, proj, proj, proj, proj]
    if has_vres:
        in_specs += [pl.BlockSpec((tl, LANES), smallblk(OFF_VRES)),
                     pl.BlockSpec((tl, gw), lambda b, g, i: (b * nt + i, g))]
        args += [proj, v_first]
    in_specs += [pl.BlockSpec((16, gw), lambda b, g, i: (0, g)),
                 pl.BlockSpec((SUBLANES, LANES), lambda b, g, i: (0, 0)),
                 pl.BlockSpec((LANES, gw), lambda b, g, i: (0, g)),
                 pl.BlockSpec((LANES, gw), lambda b, g, i: (0, g))]
    args += [pvec, mul, w2p, a2p]
    if has_vres:
        in_specs += [pl.BlockSpec((LANES, gw), lambda b, g, i: (0, g))]
        args += [v2p]
    out_blk = pl.BlockSpec((tl, gw), lambda b, g, i: (b * nt + i, g))
    out_shape = [jax.ShapeDtypeStruct((t, RWKV_WIDTH), BF16)]
    out_specs = [out_blk]
    if not has_vres:
        out_shape.append(jax.ShapeDtypeStruct((t, RWKV_WIDTH), F32))
        out_specs.append(out_blk)
    scratch = ([pltpu.VMEM((gw, gw), F32)]
               + [pltpu.VMEM((SUBLANES, gw), F32)] * 4
               + [pltpu.VMEM((SUBLANES, LANES), F32)] * 2
               + [pltpu.VMEM((tl, gw), F32)] * 7)
    res = pl.pallas_call(
        functools.partial(_rwkv_kernel, has_vres, tl // RWKV_CHUNK),
        grid=(bsz, RWKV_HEADS // HEAD_GROUP, nt),
        in_specs=in_specs, out_specs=out_specs, out_shape=out_shape,
        scratch_shapes=scratch,
        compiler_params=pltpu.CompilerParams(
            dimension_semantics=("arbitrary", "arbitrary", "arbitrary"),
            vmem_limit_bytes=VMEM_LIMIT),
        name="rwkv",
    )(*args)
    return (res[0], v_first) if has_vres else (res[0], res[1])


def _ssd_chunk(st, xs, bm, cm, dt, a_neg):
    L = SSD_CHUNK
    gw = SSM_GROUP_W
    tri = jnp.where(_iota((L, L), 1) <= _iota((L, L), 0), 1.0, 0.0).astype(BF16)
    causal = _iota((L, L), 1) <= _iota((L, L), 0)
    expand = jnp.where(_iota((LANES, gw), 0) == (_iota((LANES, gw), 1) >> 6), 1.0, 0.0).astype(BF16)

    cum = _dot3_left(tri, dt * a_neg)
    cum_last = cum[L - 1:L, :]
    wide = _dot3_right(jnp.concatenate([dt, cum, cum_last - cum], axis=0), expand)
    dt_e = wide[:L]
    ecum_e = jnp.exp(wide[L:2 * L])
    tail_e = jnp.exp(wide[2 * L:])
    xdt = xs * dt_e

    cb = _dot_nt(cm, bm)
    cum_t = cum.T
    lane_lo = _iota((L, LANES), 1) < SSM_HEAD_DIM
    pieces = []
    for pair in range(SSM_GROUP_HEADS // 2):
        x_pair = xdt[:, pair * LANES:(pair + 1) * LANES]
        outs = []
        for hh in (2 * pair, 2 * pair + 1):
            seg = cum[:, hh:hh + 1] - cum_t[hh:hh + 1, :]
            w_h = cb * jnp.exp(jnp.where(causal, seg, -jnp.inf))
            outs.append(_dot(w_h, x_pair))
        pieces.append(jnp.where(lane_lo, outs[0], outs[1]))
    y_intra = jnp.concatenate(pieces, axis=1)

    y_inter = _dot(cm, st) * ecum_e
    st1 = st * ecum_e[L - 1:L, :] + _dot_tn(bm, xdt * tail_e)
    return y_intra + y_inter, st1


def _ssd_kernel(n_chunks, z_ref, x_ref, b_ref, c_ref, dt_ref, pw_ref, ps_ref, y_ref,
                st_ref, cx, cb_c, cc_c, q_x, q_b, q_c, q_dt, q_y):
    @pl.when(pl.program_id(2) == 0)
    def _():
        st_ref[...] = jnp.zeros_like(st_ref)
        for c in (cx, cb_c, cc_c):
            c[...] = jnp.zeros_like(c)

    tl = x_ref.shape[0]
    pw = pw_ref[...]
    ps = ps_ref[...]

    def conv_silu(ref, carry, taps, bias):
        u = ref[...].astype(F32)
        c8 = carry[...]
        acc = u * taps[CONV_WIDTH - 1] + bias
        for s in range(1, CONV_WIDTH):
            acc = acc + _prev_rows(u, c8, s) * taps[CONV_WIDTH - 1 - s]
        carry[...] = u[tl - SUBLANES:tl, :]
        return acc * _sigmoid(acc)

    xs = conv_silu(x_ref, cx, [pw[i:i + 1, :] for i in range(4)], pw[4:5, :])
    q_x[...] = xs
    q_b[...] = conv_silu(b_ref, cb_c, [ps[i:i + 1, :] for i in range(4)], ps[4:5, :])
    q_c[...] = conv_silu(c_ref, cc_c, [ps[i:i + 1, :] for i in range(5, 9)], ps[9:10, :])
    q_dt[...] = _softplus(dt_ref[...].astype(F32) + ps[10:11, :])
    a_neg = -jnp.exp(ps[11:12, :])

    def chunk(c, carry):
        rows = pl.ds(pl.multiple_of(c * SSD_CHUNK, SSD_CHUNK), SSD_CHUNK)
        y, st1 = _ssd_chunk(st_ref[...], q_x[rows, :], q_b[rows, :], q_c[rows, :], q_dt[rows, :], a_neg)
        q_y[rows, :] = y
        st_ref[...] = st1
        return carry

    lax.fori_loop(0, n_chunks, chunk, 0)

    z = z_ref[...].astype(F32)
    y = (q_y[...] + pw[5:6, :] * xs) * (z * _sigmoid(z))
    y = y * lax.rsqrt(jnp.mean(y * y, axis=-1, keepdims=True) + RMS_EPS)
    y_ref[...] = (y * pw[6:7, :]).astype(y_ref.dtype)


def _ssd(proj, pwide, psmall, bsz, seq, tl):
    nt = seq // tl
    t = bsz * seq
    gw = SSM_GROUP_W
    blk = lambda off, w: (lambda b, g, i: (b * nt + i, off // w + g))
    return pl.pallas_call(
        functools.partial(_ssd_kernel, tl // SSD_CHUNK),
        grid=(bsz, SSM_GROUPS, nt),
        in_specs=[pl.BlockSpec((tl, gw), blk(OFF_SSM_Z, gw)),
                  pl.BlockSpec((tl, gw), blk(OFF_XS, gw)),
                  pl.BlockSpec((tl, LANES), blk(OFF_B, LANES)),
                  pl.BlockSpec((tl, LANES), blk(OFF_C, LANES)),
                  pl.BlockSpec((tl, LANES), blk(OFF_DT, LANES)),
                  pl.BlockSpec((16, gw), lambda b, g, i: (0, g)),
                  pl.BlockSpec((16, LANES), lambda b, g, i: (0, g))],
        out_specs=pl.BlockSpec((tl, gw), lambda b, g, i: (b * nt + i, g)),
        out_shape=jax.ShapeDtypeStruct((t, SSM_WIDTH), BF16),
        scratch_shapes=[pltpu.VMEM((SSM_STATE, gw), F32),
                        pltpu.VMEM((SUBLANES, gw), F32),
                        pltpu.VMEM((SUBLANES, LANES), F32),
                        pltpu.VMEM((SUBLANES, LANES), F32),
                        pltpu.VMEM((tl, gw), F32),
                        pltpu.VMEM((tl, LANES), F32),
                        pltpu.VMEM((tl, LANES), F32),
                        pltpu.VMEM((tl, LANES), F32),
                        pltpu.VMEM((tl, gw), F32)],
        compiler_params=pltpu.CompilerParams(
            dimension_semantics=("arbitrary", "arbitrary", "arbitrary"),
            vmem_limit_bytes=VMEM_LIMIT),
        name="ssd",
    )(proj, proj, proj, proj, proj, pwide, psmall)


def _merge_kernel(final, yr_ref, ys_ref, gr_ref, gs_ref, x_ref, wr_ref, ws_ref, wo_ref, fw_ref, o_ref):
    pr = jnp.dot(yr_ref[...], wr_ref[...], preferred_element_type=F32)
    ps = jnp.dot(ys_ref[...], ws_ref[...], preferred_element_type=F32)
    h = _sigmoid(gr_ref[...].astype(F32)) * pr + _sigmoid(gs_ref[...].astype(F32)) * ps
    o = x_ref[...] + jnp.dot(h.astype(BF16), wo_ref[...], preferred_element_type=F32)
    if final:
        o = o * lax.rsqrt(jnp.mean(o * o, axis=-1, keepdims=True) + RMS_EPS) * fw_ref[...]
    o_ref[...] = o


def _merge(y_rwkv, y_ssm, proj, x2, wr, ws, wo, fw, final, tm):
    t, d = x2.shape
    const = lambda i: (0, 0)
    return pl.pallas_call(
        functools.partial(_merge_kernel, final),
        grid=(t // tm,),
        in_specs=[pl.BlockSpec((tm, RWKV_WIDTH), lambda i: (i, 0)),
                  pl.BlockSpec((tm, SSM_WIDTH), lambda i: (i, 0)),
                  pl.BlockSpec((tm, d), lambda i: (i, OFF_G_RWKV // d)),
                  pl.BlockSpec((tm, d), lambda i: (i, OFF_G_SSM // d)),
                  pl.BlockSpec((tm, d), lambda i: (i, 0)),
                  pl.BlockSpec((RWKV_WIDTH, d), const),
                  pl.BlockSpec((SSM_WIDTH, d), const),
                  pl.BlockSpec((d, d), const),
                  pl.BlockSpec((1, d), const)],
        out_specs=pl.BlockSpec((tm, d), lambda i: (i, 0)),
        out_shape=jax.ShapeDtypeStruct((t, d), F32),
        compiler_params=pltpu.CompilerParams(
            dimension_semantics=("arbitrary",), vmem_limit_bytes=VMEM_LIMIT),
        name="merge",
    )(y_rwkv, y_ssm, proj, proj, x2, wr, ws, wo, fw)


def _pad_cols(w, width):
    return jnp.pad(w, ((0, 0), (0, width - w.shape[1])))


def _pad_rows(w, rows):
    return jnp.pad(w, ((0, rows - w.shape[0]), (0, 0)))


def _proj_weight(w_in, w_vres):
    d = w_in.shape[0]
    c = lambda a, b: w_in[:, a:b]
    rw = 0
    r, k, v = c(rw, rw + 1024), c(rw + 1024, rw + 2048), c(rw + 2048, rw + 3072)
    lora = c(rw + 3072, rw + 3200)
    z_rwkv = c(rw + 3200, rw + 4224)
    sm = 4224
    ssm_z = c(sm, sm + 2048)
    xs = c(sm + 2048, sm + 4096)
    bm = c(sm + 4096, sm + 4608)
    cm = c(sm + 4608, sm + 5120)
    dt = c(sm + 5120, sm + 5152)
    gt = sm + 5152
    g_rwkv, g_ssm = c(gt, gt + 1024), c(gt + 1024, gt + 2048)
    vres = jnp.zeros((d, LANES), w_in.dtype) if w_vres is None else _pad_cols(w_vres, LANES)
    dt4 = jnp.pad(dt.reshape(d, SSM_GROUPS, SSM_GROUP_HEADS),
                  ((0, 0), (0, 0), (0, LANES - SSM_GROUP_HEADS))).reshape(d, SSM_GROUPS * LANES)
    w = jnp.concatenate([ssm_z, g_rwkv, g_ssm, r, k, v, z_rwkv, xs, bm, cm, lora, vres, dt4], axis=1)
    return _pad_cols(w, PROJ_W).astype(BF16)


def _group_lanes(vec):
    return jnp.pad(vec.reshape(SSM_GROUPS, SSM_GROUP_HEADS),
                   ((0, 0), (0, LANES - SSM_GROUP_HEADS))).reshape(1, SSM_GROUPS * LANES)


def kernel(x, norm_w, w_in, w_in_vres, mu_rwkv, mu_vres, decay_w0, decay_w2, iclr_a0, iclr_a2, vres_v0, vres_v2, k_k, k_a, r_k, gn_w, gn_b, w_out_rwkv, conv_w, conv_b, dt_bias, a_log, d_skip, ssm_norm_w, w_out_ssm, w_out, final_norm_w):
    bsz, seq, d = x.shape
    depth = norm_w.shape[0]
    t = bsz * seq
    x2 = x.reshape(t, d)
    tm_proj = min(1024, t)
    tn_proj = 1536
    tl_rwkv = min(512, seq)
    tl_ssd = min(512, seq)
    tm_merge = min(512, t)
    proj_dtype = F32

    v_first = None
    for i in range(depth):
        w = _proj_weight(w_in[i], None if i == 0 else w_in_vres[i - 1])
        proj = _inproj(x2, norm_w[i].reshape(1, d), w, tm_proj, tn_proj, proj_dtype)

        mu = mu_rwkv[i]
        rows = [mu[0:1024], mu[1024:2048], mu[2048:3072], mu[3200:4224], decay_w0[i], iclr_a0[i],
                k_k[i], k_a[i], r_k[i].reshape(-1), gn_w[i], gn_b[i],
                vres_v0[i - 1] if i > 0 else jnp.zeros((RWKV_WIDTH,), F32)]
        pvec = _pad_rows(jnp.stack(rows, axis=0), 16)
        mu_small = jnp.stack([mu[3072:3200],
                              jnp.pad(mu_vres[i - 1], (0, LANES - VRES_LORA)) if i > 0
                              else jnp.zeros((LANES,), F32)], axis=0)
        mu_small = _pad_rows(mu_small, SUBLANES)
        w2p = _pad_rows(decay_w2[i], LANES).astype(BF16)
        a2p = jnp.concatenate([jnp.zeros_like(iclr_a2[i]), iclr_a2[i]], axis=0).astype(BF16)
        v2p = _pad_rows(vres_v2[i - 1], LANES).astype(BF16) if i > 0 else None
        y_rwkv, v_first = _rwkv(proj, v_first, pvec, mu_small, w2p, a2p, v2p, bsz, seq, tl_rwkv)

        cw, cb = conv_w[i], conv_b[i]
        pwide = jnp.concatenate([cw[:, :SSM_WIDTH], cb[None, :SSM_WIDTH],
                                 jnp.repeat(d_skip[i], SSM_HEAD_DIM)[None, :],
                                 ssm_norm_w[i][None, :]], axis=0)
        pwide = _pad_rows(pwide, 16)
        nb = SSM_GROUPS * SSM_STATE
        psmall = jnp.concatenate([cw[:, SSM_WIDTH:SSM_WIDTH + nb], cb[None, SSM_WIDTH:SSM_WIDTH + nb],
                                  cw[:, SSM_WIDTH + nb:], cb[None, SSM_WIDTH + nb:],
                                  _group_lanes(dt_bias[i]), _group_lanes(a_log[i])], axis=0)
        psmall = _pad_rows(psmall, 16)
        y_ssm = _ssd(proj, pwide, psmall, bsz, seq, tl_ssd)

        final = i == depth - 1
        x2 = _merge(y_rwkv, y_ssm, proj, x2, w_out_rwkv[i].astype(BF16), w_out_ssm[i].astype(BF16),
                    w_out[i].astype(BF16), final_norm_w.reshape(1, d), final, tm_merge)
    return x2.reshape(bsz, seq, d)
```

```python
import functools

import jax
import jax.numpy as jnp
from jax import lax
from jax.experimental import pallas as pl
from jax.experimental.pallas import tpu as pltpu

F32 = jnp.float32
BF16 = jnp.bfloat16

D_MODEL = 1024
RWKV_HEADS = 16
RWKV_HEAD_DIM = 64
RWKV_WIDTH = RWKV_HEADS * RWKV_HEAD_DIM
DECAY_LORA = 64
ICLR_LORA = 64
VRES_LORA = 32
RWKV_GN_EPS = 64e-5
SSM_WIDTH = 2 * D_MODEL
SSM_HEAD_DIM = 64
SSM_HEADS = SSM_WIDTH // SSM_HEAD_DIM
SSM_GROUPS = 4
SSM_STATE = 128
CONV_WIDTH = 4
SSD_CHUNK = 128
RMS_EPS = 1e-5

LANES = 128
SUBLANES = 8
RWKV_CHUNK = 64
HEAD_GROUP = 4
GROUP_W = HEAD_GROUP * RWKV_HEAD_DIM
SSM_GROUP_W = SSM_WIDTH // SSM_GROUPS
SSM_GROUP_HEADS = SSM_HEADS // SSM_GROUPS

OFF_SSM_Z = 0
OFF_G_RWKV = 2048
OFF_G_SSM = 3072
OFF_R = 4096
OFF_K = 5120
OFF_V = 6144
OFF_Z = 7168
OFF_XS = 8192
OFF_B = 10240
OFF_C = 10752
OFF_LORA = 11264
OFF_VRES = 11392
OFF_DT = 11520
PROJ_W = 12288

VMEM_LIMIT = 56 * 1024 * 1024


def _sigmoid(x):
    return 1.0 / (1.0 + jnp.exp(-x))


def _softplus(x):
    return jnp.maximum(x, 0.0) + jnp.log(1.0 + jnp.exp(-jnp.abs(x)))


def _dot(a, b):
    return jnp.dot(a.astype(BF16), b.astype(BF16), preferred_element_type=F32)


def _dot_nt(a, b):
    return lax.dot_general(a.astype(BF16), b.astype(BF16), (((1,), (1,)), ((), ())),
                           preferred_element_type=F32)


def _dot_tn(a, b):
    return lax.dot_general(a.astype(BF16), b.astype(BF16), (((0,), (0,)), ((), ())),
                           preferred_element_type=F32)


def _split3(x):
    hi = x.astype(BF16)
    r1 = x - hi.astype(F32)
    mid = r1.astype(BF16)
    lo = (r1 - mid.astype(F32)).astype(BF16)
    return hi, mid, lo


def _dot3_right(x, w01):
    return sum(jnp.dot(p, w01, preferred_element_type=F32) for p in _split3(x))


def _dot3_left(w01, x):
    return sum(jnp.dot(w01, p, preferred_element_type=F32) for p in _split3(x))


def _iota(shape, axis):
    return lax.broadcasted_iota(jnp.int32, shape, axis)


def _prev_rows(u, carry, s):
    us = pltpu.roll(u, s, axis=0)
    cs = pltpu.roll(carry, s, axis=0)
    head = jnp.where(_iota(carry.shape, 0) < s, cs, us[0:SUBLANES])
    return jnp.concatenate([head, us[SUBLANES:]], axis=0)


def _inproj_kernel(x_ref, nw_ref, w_ref, o_ref, xn_ref):
    @pl.when(pl.program_id(1) == 0)
    def _():
        x = x_ref[...]
        ms = jnp.mean(x * x, axis=-1, keepdims=True)
        xn_ref[...] = (x * lax.rsqrt(ms + RMS_EPS) * nw_ref[...]).astype(BF16)

    o_ref[...] = jnp.dot(xn_ref[...], w_ref[...], preferred_element_type=F32).astype(o_ref.dtype)


def _inproj(x2, nw, w, tm, tn, out_dtype):
    t, d = x2.shape
    npad = w.shape[1]
    return pl.pallas_call(
        _inproj_kernel,
        grid=(t // tm, npad // tn),
        in_specs=[
            pl.BlockSpec((tm, d), lambda i, j: (i, 0)),
            pl.BlockSpec((1, d), lambda i, j: (0, 0)),
            pl.BlockSpec((d, tn), lambda i, j: (0, j)),
        ],
        out_specs=pl.BlockSpec((tm, tn), lambda i, j: (i, j)),
        out_shape=jax.ShapeDtypeStruct((t, npad), out_dtype),
        scratch_shapes=[pltpu.VMEM((tm, d), BF16)],
        compiler_params=pltpu.CompilerParams(
            dimension_semantics=("arbitrary", "arbitrary"), vmem_limit_bytes=VMEM_LIMIT),
        name="inproj",
    )(x2, nw, w)


def _block_diag(x, lane_head):
    return jnp.concatenate(
        [jnp.where(lane_head == h, x, 0.0) for h in range(HEAD_GROUP)], axis=0)


def _rwkv_chunk_pre(chunks):
    L = RWKV_CHUNK
    W = GROUP_W
    row = _iota((L, W), 0)
    lane = _iota((L, W), 1)
    lane_head = lane >> 6
    lane_in = lane & (RWKV_HEAD_DIM - 1)
    incl = lane_in <= row
    strict = lane_in < row
    eye = jnp.where(lane_in == row, 1.0, 0.0).astype(F32)
    tri = jnp.where(_iota((L, L), 1) <= _iota((L, L), 0), 1.0, 0.0).astype(BF16)

    bd = functools.partial(_block_diag, lane_head=lane_head)
    n = range(len(chunks))
    r, k, v, ld, av, bv = ([c[i] for c in chunks] for i in range(6))

    cum = [_dot3_left(tri, ld[i]) for i in n]
    cum_last = [cum[i][L - 1:L, :] for i in n]
    r_t = [r[i] * jnp.exp(cum[i]) for i in n]
    a_t = [av[i] * jnp.exp(cum[i] - ld[i]) for i in n]
    g_inv = [jnp.exp(-cum[i]) for i in n]
    b_t = [bv[i] * g_inv[i] for i in n]
    k_t = [k[i] * g_inv[i] for i in n]
    g_tail = [jnp.exp(cum_last[i] - cum[i]) for i in n]
    b_e = [bv[i] * g_tail[i] for i in n]
    k_e = [k[i] * g_tail[i] for i in n]

    prod = [_dot_nt(jnp.concatenate([a_t[i], r_t[i]], axis=0),
                    jnp.concatenate([bd(b_t[i]), bd(k_t[i])], axis=0)) for i in n]
    m_ab = [jnp.where(strict, prod[i][:L, :W], 0.0) for i in n]
    m_ak = [jnp.where(strict, prod[i][:L, W:], 0.0) for i in n]
    p_rb = [jnp.where(incl, prod[i][L:, :W], 0.0) for i in n]
    p_rk = [jnp.where(incl, prod[i][L:, W:], 0.0) for i in n]

    x = [_dot(m_ab[i], bd(m_ab[i])) for i in n]
    t_inv = [eye + m_ab[i] for i in n]
    for _ in range(4):
        q = [_dot(jnp.concatenate([x[i], t_inv[i]], axis=0), bd(x[i])) for i in n]
        x = [q[i][:L] for i in n]
        t_inv = [t_inv[i] + q[i][L:] for i in n]
    t_inv = [t_inv[i] + _dot(t_inv[i], bd(x[i])) for i in n]

    bd_v = [bd(v[i]) for i in n]
    mv = [_dot(m_ak[i], bd_v[i]) for i in n]
    au = [_dot(t_inv[i], jnp.concatenate([bd(a_t[i]), bd(mv[i])], axis=1)) for i in n]
    ry = [_dot(p_rb[i], jnp.concatenate([bd(au[i][:, :W]), bd(au[i][:, W:])], axis=1)) for i in n]
    r_hat = [r_t[i] + ry[i][:, :W] for i in n]
    y0 = [ry[i][:, W:] + _dot(p_rk[i], bd_v[i]) for i in n]

    row2 = _iota((W, W), 0)
    lane2 = _iota((W, W), 1)
    same_head = (row2 >> 6) == (lane2 >> 6)
    diag = row2 == lane2
    g_mat = [jnp.where(same_head, _dot_tn(au[i][:, :W], b_e[i]), 0.0)
             + jnp.where(diag, jnp.exp(cum_last[i]), 0.0) for i in n]
    h_mat = [jnp.where(same_head, _dot_tn(au[i][:, W:], b_e[i]) + _dot_tn(v[i], k_e[i]), 0.0) for i in n]
    return [(r_hat[i], y0[i], g_mat[i], h_mat[i]) for i in n]


def _rwkv_kernel(has_vres, n_chunks, *refs):
    it = iter(refs)
    r_ref, k_ref, v_ref, z_ref, lora_ref = (next(it) for _ in range(5))
    vres_ref = next(it) if has_vres else None
    vf_ref = next(it) if has_vres else None
    pv_ref, mul_ref, w2_ref, a2_ref = (next(it) for _ in range(4))
    v2_ref = next(it) if has_vres else None
    y_ref = next(it)
    vout_ref = None if has_vres else next(it)
    s_ref, cr, ck, cv, cz, cl, cvr = (next(it) for _ in range(7))

    @pl.when(pl.program_id(2) == 0)
    def _():
        s_ref[...] = jnp.zeros_like(s_ref)
        for c in (cr, ck, cv, cz, cl, cvr):
            c[...] = jnp.zeros_like(c)

    tl = r_ref.shape[0]
    pv = pv_ref[...]
    prm = lambda i: pv[i:i + 1, :]
    mu_r, mu_k, mu_v, mu_z, w0, a0, k_k, k_a, r_k, gn_w, gn_b, v0 = (prm(i) for i in range(12))

    def shifted(ref, carry, mu):
        u = ref[...].astype(F32)
        prev = _prev_rows(u, carry[...], 1)
        carry[...] = u[tl - SUBLANES:tl, :]
        return u + (prev - u) * mu

    r = shifted(r_ref, cr, mu_r)
    k = shifted(k_ref, ck, mu_k)
    v = shifted(v_ref, cv, mu_v)
    lora = shifted(lora_ref, cl, mul_ref[0:1, :])

    w_log = -_softplus(-(w0 + _dot(jnp.tanh(lora), w2_ref[...]))) - 0.5
    ld = -jnp.exp(w_log)
    a = _sigmoid(a0 + _dot(lora, a2_ref[...]))

    row = _iota((GROUP_W, GROUP_W), 0)
    lane = _iota((GROUP_W, GROUP_W), 1)
    seg_ones = jnp.where((row >> 6) == (lane >> 6), 1.0, 0.0).astype(BF16)
    head_sum = lambda t: _dot3_right(t, seg_ones)

    kk = k * k_k
    kk = kk / jnp.maximum(jnp.sqrt(head_sum(kk * kk)), 1e-12)
    k = k * (1.0 + (a - 1.0) * k_a)
    if has_vres:
        vres = shifted(vres_ref, cvr, mul_ref[1:2, :])
        v = v + (vf_ref[...] - v) * _sigmoid(v0 + _dot(vres, v2_ref[...]))
    else:
        vout_ref[...] = v

    L = RWKV_CHUNK
    av = -kk
    bv = kk * a
    ins = [[t[c * L:(c + 1) * L, :] for t in (r, k, v, ld, av, bv)] for c in range(n_chunks)]
    s = s_ref[...]
    ys = []
    for r_hat, y0, g_mat, h_mat in _rwkv_chunk_pre(ins):
        ys.append(_dot_nt(r_hat, s) + y0)
        s = _dot(s, g_mat) + h_mat
    s_ref[...] = s

    y = jnp.concatenate(ys, axis=0)
    inv_n = 1.0 / RWKV_HEAD_DIM
    mean = head_sum(y) * inv_n
    yc = y - mean
    var = head_sum(yc * yc) * inv_n
    y = yc * lax.rsqrt(var + RWKV_GN_EPS) * gn_w + gn_b
    bonus = head_sum(r * k * r_k) * v
    z = shifted(z_ref, cz, mu_z)
    y_ref[...] = ((y + bonus) * (z * _sigmoid(z))).astype(y_ref.dtype)


def _rwkv(proj, v_first, pvec, mul, w2p, a2p, v2p, bsz, seq, tl):
    has_vres = v_first is not None
    nt = seq // tl
    t = bsz * seq
    gw = GROUP_W
    colblk = lambda off: (lambda b, g, i: (b * nt + i, off // gw + g))
    smallblk = lambda off: (lambda b, g, i: (b * nt + i, off // LANES))
    in_specs = [pl.BlockSpec((tl, gw), colblk(OFF_R)),
                pl.BlockSpec((tl, gw), colblk(OFF_K)),
                pl.BlockSpec((tl, gw), colblk(OFF_V)),
                pl.BlockSpec((tl, gw), colblk(OFF_Z)),
                pl.BlockSpec((tl, LANES), smallblk(OFF_LORA))]
    args = [proj, proj, proj, proj, proj]
    if has_vres:
        in_specs += [pl.BlockSpec((tl, LANES), smallblk(OFF_VRES)),
                     pl.BlockSpec((tl, gw), lambda b, g, i: (b * nt + i, g))]
        args += [proj, v_first]
    in_specs += [pl.BlockSpec((16, gw), lambda b, g, i: (0, g)),
                 pl.BlockSpec((SUBLANES, LANES), lambda b, g, i: (0, 0)),
                 pl.BlockSpec((LANES, gw), lambda b, g, i: (0, g)),
                 pl.BlockSpec((LANES, gw), lambda b, g, i: (0, g))]
    args += [pvec, mul, w2p, a2p]
    if has_vres:
        in_specs += [pl.BlockSpec((LANES, gw), lambda b, g, i: (0, g))]
        args += [v2p]
    out_blk = pl.BlockSpec((tl, gw), lambda b, g, i: (b * nt + i, g))
    out_shape = [jax.ShapeDtypeStruct((t, RWKV_WIDTH), BF16)]
    out_specs = [out_blk]
    if not has_vres:
        out_shape.append(jax.ShapeDtypeStruct((t, RWKV_WIDTH), F32))
        out_specs.append(out_blk)
    scratch = ([pltpu.VMEM((gw, gw), F32)]
               + [pltpu.VMEM((SUBLANES, gw), F32)] * 4
               + [pltpu.VMEM((SUBLANES, LANES), F32)] * 2)
    res = pl.pallas_call(
        functools.partial(_rwkv_kernel, has_vres, tl // RWKV_CHUNK),
        grid=(bsz, RWKV_HEADS // HEAD_GROUP, nt),
        in_specs=in_specs, out_specs=out_specs, out_shape=out_shape,
        scratch_shapes=scratch,
        compiler_params=pltpu.CompilerParams(
            dimension_semantics=("arbitrary", "arbitrary", "arbitrary"),
            vmem_limit_bytes=VMEM_LIMIT),
        name="rwkv",
    )(*args)
    return (res[0], v_first) if has_vres else (res[0], res[1])


def _ssd_chunk_pre(xs, bm, cm, dt, a_neg):
    L = SSD_CHUNK
    gw = SSM_GROUP_W
    tri = jnp.where(_iota((L, L), 1) <= _iota((L, L), 0), 1.0, 0.0).astype(BF16)
    causal = _iota((L, L), 1) <= _iota((L, L), 0)
    expand = jnp.where(_iota((LANES, gw), 0) == (_iota((LANES, gw), 1) >> 6), 1.0, 0.0).astype(BF16)

    cum = _dot3_left(tri, dt * a_neg)
    cum_last = cum[L - 1:L, :]
    wide = _dot3_right(jnp.concatenate([dt, cum, cum_last - cum], axis=0), expand)
    dt_e = wide[:L]
    ecum_e = jnp.exp(wide[L:2 * L])
    tail_e = jnp.exp(wide[2 * L:])
    xdt = xs * dt_e

    cb = _dot_nt(cm, bm)
    cum_t = cum.T
    lane_lo = _iota((L, LANES), 1) < SSM_HEAD_DIM
    pieces = []
    for pair in range(SSM_GROUP_HEADS // 2):
        x_pair = xdt[:, pair * LANES:(pair + 1) * LANES]
        outs = []
        for hh in (2 * pair, 2 * pair + 1):
            seg = cum[:, hh:hh + 1] - cum_t[hh:hh + 1, :]
            w_h = cb * jnp.exp(jnp.where(causal, seg, -jnp.inf))
            outs.append(_dot(w_h, x_pair))
        pieces.append(jnp.where(lane_lo, outs[0], outs[1]))
    y_intra = jnp.concatenate(pieces, axis=1)

    return y_intra, ecum_e, _dot_tn(bm, xdt * tail_e)


def _ssd_kernel(n_chunks, z_ref, x_ref, b_ref, c_ref, dt_ref, pw_ref, ps_ref, y_ref,
                st_ref, cx, cb_c, cc_c):
    @pl.when(pl.program_id(2) == 0)
    def _():
        st_ref[...] = jnp.zeros_like(st_ref)
        for c in (cx, cb_c, cc_c):
            c[...] = jnp.zeros_like(c)

    tl = x_ref.shape[0]
    pw = pw_ref[...]
    ps = ps_ref[...]

    def conv_silu(ref, carry, taps, bias):
        u = ref[...].astype(F32)
        c8 = carry[...]
        acc = u * taps[CONV_WIDTH - 1] + bias
        for s in range(1, CONV_WIDTH):
            acc = acc + _prev_rows(u, c8, s) * taps[CONV_WIDTH - 1 - s]
        carry[...] = u[tl - SUBLANES:tl, :]
        return acc * _sigmoid(acc)

    xs = conv_silu(x_ref, cx, [pw[i:i + 1, :] for i in range(4)], pw[4:5, :])
    bm = conv_silu(b_ref, cb_c, [ps[i:i + 1, :] for i in range(4)], ps[4:5, :])
    cm = conv_silu(c_ref, cc_c, [ps[i:i + 1, :] for i in range(5, 9)], ps[9:10, :])
    dt = _softplus(dt_ref[...].astype(F32) + ps[10:11, :])
    a_neg = -jnp.exp(ps[11:12, :])

    L = SSD_CHUNK
    sl = lambda t, c: t[c * L:(c + 1) * L, :]
    pre = [_ssd_chunk_pre(sl(xs, c), sl(bm, c), sl(cm, c), sl(dt, c), a_neg) for c in range(n_chunks)]
    st = st_ref[...]
    ys = []
    for c, (y_intra, ecum_e, upd) in enumerate(pre):
        ys.append(y_intra + _dot(sl(cm, c), st) * ecum_e)
        st = st * ecum_e[L - 1:L, :] + upd
    st_ref[...] = st

    z = z_ref[...].astype(F32)
    y = (jnp.concatenate(ys, axis=0) + pw[5:6, :] * xs) * (z * _sigmoid(z))
    y = y * lax.rsqrt(jnp.mean(y * y, axis=-1, keepdims=True) + RMS_EPS)
    y_ref[...] = (y * pw[6:7, :]).astype(y_ref.dtype)


def _ssd(proj, pwide, psmall, bsz, seq, tl):
    nt = seq // tl
    t = bsz * seq
    gw = SSM_GROUP_W
    blk = lambda off, w: (lambda b, g, i: (b * nt + i, off // w + g))
    return pl.pallas_call(
        functools.partial(_ssd_kernel, tl // SSD_CHUNK),
        grid=(bsz, SSM_GROUPS, nt),
        in_specs=[pl.BlockSpec((tl, gw), blk(OFF_SSM_Z, gw)),
                  pl.BlockSpec((tl, gw), blk(OFF_XS, gw)),
                  pl.BlockSpec((tl, LANES), blk(OFF_B, LANES)),
                  pl.BlockSpec((tl, LANES), blk(OFF_C, LANES)),
                  pl.BlockSpec((tl, LANES), blk(OFF_DT, LANES)),
                  pl.BlockSpec((16, gw), lambda b, g, i: (0, g)),
                  pl.BlockSpec((16, LANES), lambda b, g, i: (0, g))],
        out_specs=pl.BlockSpec((tl, gw), lambda b, g, i: (b * nt + i, g)),
        out_shape=jax.ShapeDtypeStruct((t, SSM_WIDTH), BF16),
        scratch_shapes=[pltpu.VMEM((SSM_STATE, gw), F32),
                        pltpu.VMEM((SUBLANES, gw), F32),
                        pltpu.VMEM((SUBLANES, LANES), F32),
                        pltpu.VMEM((SUBLANES, LANES), F32)],
        compiler_params=pltpu.CompilerParams(
            dimension_semantics=("arbitrary", "arbitrary", "arbitrary"),
            vmem_limit_bytes=VMEM_LIMIT),
        name="ssd",
    )(proj, proj, proj, proj, proj, pwide, psmall)


def _merge_kernel(final, yr_ref, ys_ref, gr_ref, gs_ref, x_ref, wr_ref, ws_ref, wo_ref, fw_ref, o_ref):
    pr = jnp.dot(yr_ref[...], wr_ref[...], preferred_element_type=F32)
    ps = jnp.dot(ys_ref[...], ws_ref[...], preferred_element_type=F32)
    h = _sigmoid(gr_ref[...].astype(F32)) * pr + _sigmoid(gs_ref[...].astype(F32)) * ps
    o = x_ref[...] + jnp.dot(h.astype(BF16), wo_ref[...], preferred_element_type=F32)
    if final:
        o = o * lax.rsqrt(jnp.mean(o * o, axis=-1, keepdims=True) + RMS_EPS) * fw_ref[...]
    o_ref[...] = o


def _merge(y_rwkv, y_ssm, proj, x2, wr, ws, wo, fw, final, tm):
    t, d = x2.shape
    const = lambda i: (0, 0)
    return pl.pallas_call(
        functools.partial(_merge_kernel, final),
        grid=(t // tm,),
        in_specs=[pl.BlockSpec((tm, RWKV_WIDTH), lambda i: (i, 0)),
                  pl.BlockSpec((tm, SSM_WIDTH), lambda i: (i, 0)),
                  pl.BlockSpec((tm, d), lambda i: (i, OFF_G_RWKV // d)),
                  pl.BlockSpec((tm, d), lambda i: (i, OFF_G_SSM // d)),
                  pl.BlockSpec((tm, d), lambda i: (i, 0)),
                  pl.BlockSpec((RWKV_WIDTH, d), const),
                  pl.BlockSpec((SSM_WIDTH, d), const),
                  pl.BlockSpec((d, d), const),
                  pl.BlockSpec((1, d), const)],
        out_specs=pl.BlockSpec((tm, d), lambda i: (i, 0)),
        out_shape=jax.ShapeDtypeStruct((t, d), F32),
        compiler_params=pltpu.CompilerParams(
            dimension_semantics=("arbitrary",), vmem_limit_bytes=VMEM_LIMIT),
        name="merge",
    )(y_rwkv, y_ssm, proj, proj, x2, wr, ws, wo, fw)


def _pad_cols(w, width):
    return jnp.pad(w, ((0, 0), (0, width - w.shape[1])))


def _pad_rows(w, rows):
    return jnp.pad(w, ((0, rows - w.shape[0]), (0, 0)))


def _proj_weight(w_in, w_vres):
    d = w_in.shape[0]
    c = lambda a, b: w_in[:, a:b]
    rw = 0
    r, k, v = c(rw, rw + 1024), c(rw + 1024, rw + 2048), c(rw + 2048, rw + 3072)
    lora = c(rw + 3072, rw + 3200)
    z_rwkv = c(rw + 3200, rw + 4224)
    sm = 4224
    ssm_z = c(sm, sm + 2048)
    xs = c(sm + 2048, sm + 4096)
    bm = c(sm + 4096, sm + 4608)
    cm = c(sm + 4608, sm + 5120)
    dt = c(sm + 5120, sm + 5152)
    gt = sm + 5152
    g_rwkv, g_ssm = c(gt, gt + 1024), c(gt + 1024, gt + 2048)
    vres = jnp.zeros((d, LANES), w_in.dtype) if w_vres is None else _pad_cols(w_vres, LANES)
    dt4 = jnp.pad(dt.reshape(d, SSM_GROUPS, SSM_GROUP_HEADS),
                  ((0, 0), (0, 0), (0, LANES - SSM_GROUP_HEADS))).reshape(d, SSM_GROUPS * LANES)
    w = jnp.concatenate([ssm_z, g_rwkv, g_ssm, r, k, v, z_rwkv, xs, bm, cm, lora, vres, dt4], axis=1)
    return _pad_cols(w, PROJ_W).astype(BF16)


def _group_lanes(vec):
    return jnp.pad(vec.reshape(SSM_GROUPS, SSM_GROUP_HEADS),
                   ((0, 0), (0, LANES - SSM_GROUP_HEADS))).reshape(1, SSM_GROUPS * LANES)


def kernel(x, norm_w, w_in, w_in_vres, mu_rwkv, mu_vres, decay_w0, decay_w2, iclr_a0, iclr_a2, vres_v0, vres_v2, k_k, k_a, r_k, gn_w, gn_b, w_out_rwkv, conv_w, conv_b, dt_bias, a_log, d_skip, ssm_norm_w, w_out_ssm, w_out, final_norm_w):
    bsz, seq, d = x.shape
    depth = norm_w.shape[0]
    t = bsz * seq
    x2 = x.reshape(t, d)
    tm_proj = min(1024, t)
    tn_proj = 1536
    tl_rwkv = min(512, seq)
    tl_ssd = min(512, seq)
    tm_merge = min(512, t)
    proj_dtype = F32

    v_first = None
    for i in range(depth):
        w = _proj_weight(w_in[i], None if i == 0 else w_in_vres[i - 1])
        proj = _inproj(x2, norm_w[i].reshape(1, d), w, tm_proj, tn_proj, proj_dtype)

        mu = mu_rwkv[i]
        rows = [mu[0:1024], mu[1024:2048], mu[2048:3072], mu[3200:4224], decay_w0[i], iclr_a0[i],
                k_k[i], k_a[i], r_k[i].reshape(-1), gn_w[i], gn_b[i],
                vres_v0[i - 1] if i > 0 else jnp.zeros((RWKV_WIDTH,), F32)]
        pvec = _pad_rows(jnp.stack(rows, axis=0), 16)
        mu_small = jnp.stack([mu[3072:3200],
                              jnp.pad(mu_vres[i - 1], (0, LANES - VRES_LORA)) if i > 0
                              else jnp.zeros((LANES,), F32)], axis=0)
        mu_small = _pad_rows(mu_small, SUBLANES)
        w2p = _pad_rows(decay_w2[i], LANES).astype(BF16)
        a2p = jnp.concatenate([jnp.zeros_like(iclr_a2[i]), iclr_a2[i]], axis=0).astype(BF16)
        v2p = _pad_rows(vres_v2[i - 1], LANES).astype(BF16) if i > 0 else None
        y_rwkv, v_first = _rwkv(proj, v_first, pvec, mu_small, w2p, a2p, v2p, bsz, seq, tl_rwkv)

        cw, cb = conv_w[i], conv_b[i]
        pwide = jnp.concatenate([cw[:, :SSM_WIDTH], cb[None, :SSM_WIDTH],
                                 jnp.repeat(d_skip[i], SSM_HEAD_DIM)[None, :],
                                 ssm_norm_w[i][None, :]], axis=0)
        pwide = _pad_rows(pwide, 16)
        nb = SSM_GROUPS * SSM_STATE
        psmall = jnp.concatenate([cw[:, SSM_WIDTH:SSM_WIDTH + nb], cb[None, SSM_WIDTH:SSM_WIDTH + nb],
                                  cw[:, SSM_WIDTH + nb:], cb[None, SSM_WIDTH + nb:],
                                  _group_lanes(dt_bias[i]), _group_lanes(a_log[i])], axis=0)
        psmall = _pad_rows(psmall, 16)
        y_ssm = _ssd(proj, pwide, psmall, bsz, seq, tl_ssd)

        final = i == depth - 1
        x2 = _merge(y_rwkv, y_ssm, proj, x2, w_out_rwkv[i].astype(BF16), w_out_ssm[i].astype(BF16),
                    w_out[i].astype(BF16), final_norm_w.reshape(1, d), final, tm_merge)
    return x2.reshape(bsz, seq, d)
```

```python
import functools

import jax
import jax.numpy as jnp
from jax import lax
from jax.experimental import pallas as pl
from jax.experimental.pallas import tpu as pltpu

F32 = jnp.float32
BF16 = jnp.bfloat16

D_MODEL = 1024
RWKV_HEADS = 16
RWKV_HEAD_DIM = 64
RWKV_WIDTH = RWKV_HEADS * RWKV_HEAD_DIM
DECAY_LORA = 64
ICLR_LORA = 64
VRES_LORA = 32
RWKV_GN_EPS = 64e-5
SSM_WIDTH = 2 * D_MODEL
SSM_HEAD_DIM = 64
SSM_HEADS = SSM_WIDTH // SSM_HEAD_DIM
SSM_GROUPS = 4
SSM_STATE = 128
CONV_WIDTH = 4
SSD_CHUNK = 128
RMS_EPS = 1e-5

LANES = 128
SUBLANES = 8
DECAY_SCALE = 0.6065306597126334
RWKV_CHUNK = 64
HEAD_GROUP = 4
GROUP_W = HEAD_GROUP * RWKV_HEAD_DIM
SSM_GROUP_W = SSM_WIDTH // SSM_GROUPS
SSM_GROUP_HEADS = SSM_HEADS // SSM_GROUPS

OFF_SSM_Z = 0
OFF_G_RWKV = 2048
OFF_G_SSM = 3072
OFF_R = 4096
OFF_K = 5120
OFF_V = 6144
OFF_Z = 7168
OFF_XS = 8192
OFF_B = 10240
OFF_C = 10752
OFF_LORA = 11264
OFF_VRES = 11392
OFF_DT = 11520
PROJ_W = 12288

VMEM_LIMIT = 56 * 1024 * 1024


def _sigmoid(x):
    return 1.0 / (1.0 + jnp.exp(-x))


def _softplus(x):
    return jnp.maximum(x, 0.0) + jnp.log(1.0 + jnp.exp(-jnp.abs(x)))


def _dot(a, b):
    return jnp.dot(a.astype(BF16), b.astype(BF16), preferred_element_type=F32)


def _dot_nt(a, b):
    return lax.dot_general(a.astype(BF16), b.astype(BF16), (((1,), (1,)), ((), ())),
                           preferred_element_type=F32)


def _dot_tn(a, b):
    return lax.dot_general(a.astype(BF16), b.astype(BF16), (((0,), (0,)), ((), ())),
                           preferred_element_type=F32)


def _split(x, passes):
    parts = []
    for _ in range(passes - 1):
        p = x.astype(BF16)
        parts.append(p)
        x = x - p.astype(F32)
    parts.append(x.astype(BF16))
    return parts


def _dot_right01(x, w01, passes):
    return sum(jnp.dot(p, w01, preferred_element_type=F32) for p in _split(x, passes))


def _dot_left01(w01, x, passes):
    return sum(jnp.dot(w01, p, preferred_element_type=F32) for p in _split(x, passes))


def _iota(shape, axis):
    return lax.broadcasted_iota(jnp.int32, shape, axis)


def _delayed_rows(ext_ref, u, delays, advance=None):
    tl = u.shape[0]
    ext_ref[SUBLANES:, :] = u
    out = [ext_ref[SUBLANES - s:SUBLANES - s + tl, :] for s in delays]
    last = u[tl - SUBLANES:tl, :]
    ext_ref[0:SUBLANES, :] = last if advance is None else jnp.where(advance, last, ext_ref[0:SUBLANES, :])
    return out


def _interleave(rounds, *gens_and_paces):
    done = [0.0] * len(gens_and_paces)
    for _ in range(rounds):
        for j, (g, pace) in enumerate(gens_and_paces):
            done[j] += pace
            while done[j] >= 1.0:
                done[j] -= 1.0
                next(g, None)
    for g, _ in gens_and_paces:
        for _ in g:
            pass


def _inproj_kernel(x_ref, nw_ref, w_ref, o_ref, xn_ref):
    @pl.when(pl.program_id(1) == 0)
    def _():
        x = x_ref[...]
        ms = jnp.mean(x * x, axis=-1, keepdims=True)
        xn_ref[...] = (x * lax.rsqrt(ms + RMS_EPS) * nw_ref[...]).astype(BF16)

    o_ref[...] = jnp.dot(xn_ref[...], w_ref[...], preferred_element_type=F32).astype(o_ref.dtype)


def _inproj(x2, nw, w, tm, tn, out_dtype):
    t, d = x2.shape
    npad = w.shape[1]
    return pl.pallas_call(
        _inproj_kernel,
        grid=(t // tm, npad // tn),
        in_specs=[
            pl.BlockSpec((tm, d), lambda i, j: (i, 0)),
            pl.BlockSpec((1, d), lambda i, j: (0, 0)),
            pl.BlockSpec((d, tn), lambda i, j: (0, j)),
        ],
        out_specs=pl.BlockSpec((tm, tn), lambda i, j: (i, j)),
        out_shape=jax.ShapeDtypeStruct((t, npad), out_dtype),
        scratch_shapes=[pltpu.VMEM((tm, d), BF16)],
        compiler_params=pltpu.CompilerParams(
            dimension_semantics=("arbitrary", "arbitrary"), vmem_limit_bytes=VMEM_LIMIT),
        name="inproj",
    )(x2, nw, w)


def _block_diag(x, lane_head):
    return jnp.concatenate(
        [jnp.where(lane_head == h, x, 0.0) for h in range(HEAD_GROUP)], axis=0)


def _rwkv_chunk_pre(chunks, out):
    L = RWKV_CHUNK
    W = GROUP_W
    row = _iota((L, W), 0)
    lane = _iota((L, W), 1)
    lane_head = lane >> 6
    lane_in = lane & (RWKV_HEAD_DIM - 1)
    incl = lane_in <= row
    strict = lane_in < row
    eye = jnp.where(lane_in == row, 1.0, 0.0).astype(F32)
    tri = jnp.where(_iota((L, L), 1) <= _iota((L, L), 0), 1.0, 0.0).astype(BF16)

    bd = functools.partial(_block_diag, lane_head=lane_head)
    n = range(len(chunks))
    r, k, v, ld, av, bv = ([c[i] for c in chunks] for i in range(6))

    cum = [_dot_left01(tri, ld[i], 2) for i in n]
    g_last = [jnp.exp(cum[i][L - 1:L, :]) for i in n]
    r_t = [r[i] * jnp.exp(cum[i]) for i in n]
    a_t = [av[i] * jnp.exp(cum[i] - ld[i]) for i in n]
    g_inv = [jnp.exp(-cum[i]) for i in n]
    b_t = [bv[i] * g_inv[i] for i in n]
    k_t = [k[i] * g_inv[i] for i in n]
    g_tail = [g_last[i] * g_inv[i] for i in n]
    b_e = [bv[i] * g_tail[i] for i in n]
    k_e = [k[i] * g_tail[i] for i in n]
    yield

    prod = [_dot_nt(jnp.concatenate([a_t[i], r_t[i]], axis=0),
                    jnp.concatenate([bd(b_t[i]), bd(k_t[i])], axis=0)) for i in n]
    m_ab = [jnp.where(strict, prod[i][:L, :W], 0.0) for i in n]
    m_ak = [jnp.where(strict, prod[i][:L, W:], 0.0) for i in n]
    p_rb = [jnp.where(incl, prod[i][L:, :W], 0.0) for i in n]
    p_rk = [jnp.where(incl, prod[i][L:, W:], 0.0) for i in n]
    yield

    x = [_dot(m_ab[i], bd(m_ab[i])) for i in n]
    t_inv = [eye + m_ab[i] for i in n]
    mp_v = [_dot(jnp.concatenate([m_ak[i], p_rk[i]], axis=0), bd(v[i])) for i in n]
    yield
    for _ in range(4):
        q = [_dot(jnp.concatenate([x[i], t_inv[i]], axis=0), bd(x[i])) for i in n]
        x = [q[i][:L] for i in n]
        t_inv = [t_inv[i] + q[i][L:] for i in n]
        yield
    t_inv = [t_inv[i] + _dot(t_inv[i], bd(x[i])) for i in n]
    yield

    au = [_dot(t_inv[i], jnp.concatenate([bd(a_t[i]), bd(mp_v[i][:L])], axis=1)) for i in n]
    yield
    ry = [_dot(p_rb[i], jnp.concatenate([bd(au[i][:, :W]), bd(au[i][:, W:])], axis=1)) for i in n]
    r_hat = [r_t[i] + ry[i][:, :W] for i in n]
    y0 = [ry[i][:, W:] + mp_v[i][L:] for i in n]
    out.extend(dict(ra=jnp.concatenate([r_hat[i], au[i][:, :W]], axis=0), y0=y0[i], u0=au[i][:, W:],
                    v=v[i], bk_e=jnp.concatenate([b_e[i], k_e[i]], axis=0),
                    g_last=g_last[i]) for i in n)


def _rwkv_chunk_state(s, c, same_head):
    L = RWKV_CHUNK
    ra = _dot_nt(c["ra"], s)
    yield None
    y = ra[:L] + c["y0"]
    uv = jnp.concatenate([(ra[L:] + c["u0"]).astype(BF16), c["v"].astype(BF16)], axis=0)
    s = s * c["g_last"] + jnp.where(same_head, _dot_tn(uv, c["bk_e"]), 0.0)
    yield y, s


def _rwkv_kernel(has_vres, n_chunks, *refs):
    it = iter(refs)
    r_ref, k_ref, v_ref, z_ref, lora_ref = (next(it) for _ in range(5))
    vres_ref = next(it) if has_vres else None
    vf_ref = next(it) if has_vres else None
    pv_ref, mul_ref, w2_ref, a2_ref = (next(it) for _ in range(4))
    v2_ref = next(it) if has_vres else None
    y_ref = next(it)
    vout_ref = None if has_vres else next(it)
    s_ref, cr, ck, cv, cz, cl, cvr = (next(it) for _ in range(7))
    st_ra, st_y0, st_u0, st_v, st_bke, st_gl, st_bon, st_zg = (next(it) for _ in range(8))
    stash = (st_ra, st_y0, st_u0, st_v, st_bke, st_gl, st_bon, st_zg)

    t = pl.program_id(2)
    nt = pl.num_programs(2) - 1
    wslot = t % 2
    rslot = 1 - wslot
    advance = t < nt - 1

    @pl.when(t == 0)
    def _():
        s_ref[...] = jnp.zeros_like(s_ref)
        for c in (cr, ck, cv, cz, cl, cvr):
            c[0:SUBLANES, :] = jnp.zeros((SUBLANES, c.shape[1]), c.dtype)
        for st in stash:
            st[1] = jnp.zeros(st.shape[1:], st.dtype)

    pv = pv_ref[...]
    prm = lambda i: pv[i:i + 1, :]
    mu_r, mu_k, mu_v, mu_z, w0, a0, k_k, k_a, r_k, gn_w, gn_b, v0 = (prm(i) for i in range(12))

    row = _iota((GROUP_W, GROUP_W), 0)
    lane = _iota((GROUP_W, GROUP_W), 1)
    same_head = (row >> 6) == (lane >> 6)
    seg_ones = jnp.where(same_head, 1.0, 0.0).astype(BF16)
    head_sum = lambda x: _dot_right01(x, seg_ones, 2)
    L = RWKV_CHUNK

    def shifted(ref, ext, mu):
        u = ref[...].astype(F32)
        prev, = _delayed_rows(ext, u, (1,), advance)
        return u + (prev - u) * mu

    def current_block():
        r = shifted(r_ref, cr, mu_r)
        k = shifted(k_ref, ck, mu_k)
        yield
        v = shifted(v_ref, cv, mu_v)
        lora = shifted(lora_ref, cl, mul_ref[0:1, :])
        yield
        ld = -DECAY_SCALE * _sigmoid(w0 + _dot(jnp.tanh(lora), w2_ref[...]))
        a = _sigmoid(a0 + _dot(lora, a2_ref[...]))
        yield
        kk = k * k_k
        kk = kk * lax.rsqrt(jnp.maximum(head_sum(kk * kk), 1e-24))
        k = k * (1.0 + (a - 1.0) * k_a)
        yield
        if has_vres:
            vres = shifted(vres_ref, cvr, mul_ref[1:2, :])
            v = v + (vf_ref[...] - v) * _sigmoid(v0 + _dot(vres, v2_ref[...]))
        else:
            vout_ref[...] = v
        yield
        z = shifted(z_ref, cz, mu_z)
        st_zg[wslot] = z * _sigmoid(z)
        st_bon[wslot] = head_sum(r * k * r_k) * v
        yield
        av = -kk
        bv = kk * a
        ins = [[x[c * L:(c + 1) * L, :] for x in (r, k, v, ld, av, bv)] for c in range(n_chunks)]
        out = []
        yield from _rwkv_chunk_pre(ins, out)
        for c, o in enumerate(out):
            st_ra[wslot, 2 * c * L:2 * (c + 1) * L, :] = o["ra"].astype(BF16)
            st_bke[wslot, 2 * c * L:2 * (c + 1) * L, :] = o["bk_e"].astype(BF16)
            st_y0[wslot, c * L:(c + 1) * L, :] = o["y0"]
            st_u0[wslot, c * L:(c + 1) * L, :] = o["u0"]
            st_v[wslot, c * L:(c + 1) * L, :] = o["v"].astype(BF16)
            st_gl[wslot, c * SUBLANES:(c + 1) * SUBLANES, :] = jnp.broadcast_to(
                o["g_last"], (SUBLANES, GROUP_W))

    def previous_block():
        s = s_ref[...]
        ys = []
        for c in range(n_chunks):
            cd = dict(ra=st_ra[rslot, 2 * c * L:2 * (c + 1) * L, :],
                      bk_e=st_bke[rslot, 2 * c * L:2 * (c + 1) * L, :],
                      y0=st_y0[rslot, c * L:(c + 1) * L, :], u0=st_u0[rslot, c * L:(c + 1) * L, :],
                      v=st_v[rslot, c * L:(c + 1) * L, :],
                      g_last=st_gl[rslot, c * SUBLANES:c * SUBLANES + 1, :])
            for res in _rwkv_chunk_state(s, cd, same_head):
                yield
            y, s = res
            ys.append(y)
        s_ref[...] = s
        y = jnp.concatenate(ys, axis=0)
        inv_n = 1.0 / RWKV_HEAD_DIM
        mean = head_sum(y) * inv_n
        yc = y - mean
        yield
        var = head_sum(yc * yc) * inv_n
        y = yc * lax.rsqrt(var + RWKV_GN_EPS) * gn_w + gn_b
        y_ref[...] = ((y + st_bon[rslot]) * st_zg[rslot]).astype(y_ref.dtype)

    _interleave(20, (current_block(), 1.0), (previous_block(), 1.0))


def _rwkv(proj, v_first, pvec, mul, w2p, a2p, v2p, bsz, seq, tl):
    has_vres = v_first is not None
    nt = seq // tl
    t = bsz * seq
    gw = GROUP_W
    cur = lambda b, i: b * nt + jnp.minimum(i, nt - 1)
    prv = lambda b, i: b * nt + jnp.maximum(i - 1, 0)
    colblk = lambda off: (lambda b, g, i: (cur(b, i), off // gw + g))
    smallblk = lambda off: (lambda b, g, i: (cur(b, i), off // LANES))
    in_specs = [pl.BlockSpec((tl, gw), colblk(OFF_R)),
                pl.BlockSpec((tl, gw), colblk(OFF_K)),
                pl.BlockSpec((tl, gw), colblk(OFF_V)),
                pl.BlockSpec((tl, gw), colblk(OFF_Z)),
                pl.BlockSpec((tl, LANES), smallblk(OFF_LORA))]
    args = [proj, proj, proj, proj, proj]
    if has_vres:
        in_specs += [pl.BlockSpec((tl, LANES), smallblk(OFF_VRES)),
                     pl.BlockSpec((tl, gw), lambda b, g, i: (cur(b, i), g))]
        args += [proj, v_first]
    in_specs += [pl.BlockSpec((16, gw), lambda b, g, i: (0, g)),
                 pl.BlockSpec((SUBLANES, LANES), lambda b, g, i: (0, 0)),
                 pl.BlockSpec((LANES, gw), lambda b, g, i: (0, g)),
                 pl.BlockSpec((LANES, gw), lambda b, g, i: (0, g))]
    args += [pvec, mul, w2p, a2p]
    if has_vres:
        in_specs += [pl.BlockSpec((LANES, gw), lambda b, g, i: (0, g))]
        args += [v2p]
    out_shape = [jax.ShapeDtypeStruct((t, RWKV_WIDTH), BF16)]
    out_specs = [pl.BlockSpec((tl, gw), lambda b, g, i: (prv(b, i), g))]
    if not has_vres:
        out_shape.append(jax.ShapeDtypeStruct((t, RWKV_WIDTH), F32))
        out_specs.append(pl.BlockSpec((tl, gw), lambda b, g, i: (cur(b, i), g)))
    n_chunks = tl // RWKV_CHUNK
    stash = lambda rows, dtype: pltpu.VMEM((2, rows, gw), dtype)
    scratch = ([pltpu.VMEM((gw, gw), F32)]
               + [pltpu.VMEM((SUBLANES + tl, gw), F32)] * 4
               + [pltpu.VMEM((SUBLANES + tl, LANES), F32)] * 2
               + [stash(2 * tl, BF16), stash(tl, F32), stash(tl, F32), stash(tl, BF16),
                  stash(2 * tl, BF16), stash(SUBLANES * n_chunks, F32), stash(tl, F32), stash(tl, F32)])
    res = pl.pallas_call(
        functools.partial(_rwkv_kernel, has_vres, n_chunks),
        grid=(bsz, RWKV_HEADS // HEAD_GROUP, nt + 1),
        in_specs=in_specs, out_specs=out_specs, out_shape=out_shape,
        scratch_shapes=scratch,
        compiler_params=pltpu.CompilerParams(
            dimension_semantics=("arbitrary", "arbitrary", "arbitrary"),
            vmem_limit_bytes=VMEM_LIMIT),
        name="rwkv",
    )(*args)
    return (res[0], v_first) if has_vres else (res[0], res[1])


def _ssd_chunk_pre(xs, bm, cm, dt, a_neg):
    L = SSD_CHUNK
    gw = SSM_GROUP_W
    tri = jnp.where(_iota((L, L), 1) <= _iota((L, L), 0), 1.0, 0.0).astype(BF16)
    causal = _iota((L, L), 1) <= _iota((L, L), 0)
    expand = jnp.where(_iota((LANES, gw), 0) == (_iota((LANES, gw), 1) >> 6), 1.0, 0.0).astype(BF16)

    cum = _dot_left01(tri, dt * a_neg, 3)
    cum_last = cum[L - 1:L, :]
    wide = _dot_right01(jnp.concatenate([dt, cum, cum_last - cum], axis=0), expand, 3)
    dt_e = wide[:L]
    ecum_e = jnp.exp(wide[L:2 * L])
    tail_e = jnp.exp(wide[2 * L:])
    xdt = xs * dt_e

    cb = _dot_nt(cm, bm)
    cum_t = cum.T
    lane_lo = _iota((L, LANES), 1) < SSM_HEAD_DIM
    pieces = []
    for pair in range(SSM_GROUP_HEADS // 2):
        x_pair = xdt[:, pair * LANES:(pair + 1) * LANES]
        outs = []
        for hh in (2 * pair, 2 * pair + 1):
            seg = cum[:, hh:hh + 1] - cum_t[hh:hh + 1, :]
            w_h = cb * jnp.exp(jnp.where(causal, seg, -jnp.inf))
            outs.append(_dot(w_h, x_pair))
        pieces.append(jnp.where(lane_lo, outs[0], outs[1]))
    y_intra = jnp.concatenate(pieces, axis=1)

    return y_intra, ecum_e, _dot_tn(bm, xdt * tail_e)


def _ssd_kernel(n_chunks, z_ref, x_ref, b_ref, c_ref, dt_ref, pw_ref, ps_ref, y_ref,
                st_ref, cx, cb_c, cc_c):
    @pl.when(pl.program_id(2) == 0)
    def _():
        st_ref[...] = jnp.zeros_like(st_ref)
        for c in (cx, cb_c, cc_c):
            c[0:SUBLANES, :] = jnp.zeros((SUBLANES, c.shape[1]), c.dtype)

    pw = pw_ref[...]
    ps = ps_ref[...]

    def conv_silu(ref, ext, taps, bias):
        u = ref[...].astype(F32)
        acc = u * taps[CONV_WIDTH - 1] + bias
        delays = tuple(range(1, CONV_WIDTH))
        for s, us in zip(delays, _delayed_rows(ext, u, delays)):
            acc = acc + us * taps[CONV_WIDTH - 1 - s]
        return acc * _sigmoid(acc)

    xs = conv_silu(x_ref, cx, [pw[i:i + 1, :] for i in range(4)], pw[4:5, :])
    bm = conv_silu(b_ref, cb_c, [ps[i:i + 1, :] for i in range(4)], ps[4:5, :])
    cm = conv_silu(c_ref, cc_c, [ps[i:i + 1, :] for i in range(5, 9)], ps[9:10, :])
    dt = _softplus(dt_ref[...].astype(F32) + ps[10:11, :])
    a_neg = -jnp.exp(ps[11:12, :])

    L = SSD_CHUNK
    sl = lambda t, c: t[c * L:(c + 1) * L, :]
    pre = [_ssd_chunk_pre(sl(xs, c), sl(bm, c), sl(cm, c), sl(dt, c), a_neg) for c in range(n_chunks)]
    st = st_ref[...]
    ys = []
    for c, (y_intra, ecum_e, upd) in enumerate(pre):
        ys.append(y_intra + _dot(sl(cm, c), st) * ecum_e)
        st = st * ecum_e[L - 1:L, :] + upd
    st_ref[...] = st

    z = z_ref[...].astype(F32)
    y = (jnp.concatenate(ys, axis=0) + pw[5:6, :] * xs) * (z * _sigmoid(z))
    y = y * lax.rsqrt(jnp.mean(y * y, axis=-1, keepdims=True) + RMS_EPS)
    y_ref[...] = (y * pw[6:7, :]).astype(y_ref.dtype)


def _ssd(proj, pwide, psmall, bsz, seq, tl):
    nt = seq // tl
    t = bsz * seq
    gw = SSM_GROUP_W
    blk = lambda off, w: (lambda b, g, i: (b * nt + i, off // w + g))
    return pl.pallas_call(
        functools.partial(_ssd_kernel, tl // SSD_CHUNK),
        grid=(bsz, SSM_GROUPS, nt),
        in_specs=[pl.BlockSpec((tl, gw), blk(OFF_SSM_Z, gw)),
                  pl.BlockSpec((tl, gw), blk(OFF_XS, gw)),
                  pl.BlockSpec((tl, LANES), blk(OFF_B, LANES)),
                  pl.BlockSpec((tl, LANES), blk(OFF_C, LANES)),
                  pl.BlockSpec((tl, LANES), blk(OFF_DT, LANES)),
                  pl.BlockSpec((16, gw), lambda b, g, i: (0, g)),
                  pl.BlockSpec((16, LANES), lambda b, g, i: (0, g))],
        out_specs=pl.BlockSpec((tl, gw), lambda b, g, i: (b * nt + i, g)),
        out_shape=jax.ShapeDtypeStruct((t, SSM_WIDTH), BF16),
        scratch_shapes=[pltpu.VMEM((SSM_STATE, gw), F32),
                        pltpu.VMEM((SUBLANES + tl, gw), F32),
                        pltpu.VMEM((SUBLANES + tl, LANES), F32),
                        pltpu.VMEM((SUBLANES + tl, LANES), F32)],
        compiler_params=pltpu.CompilerParams(
            dimension_semantics=("arbitrary", "arbitrary", "arbitrary"),
            vmem_limit_bytes=VMEM_LIMIT),
        name="ssd",
    )(proj, proj, proj, proj, proj, pwide, psmall)


def _merge_kernel(final, yr_ref, ys_ref, gr_ref, gs_ref, x_ref, wr_ref, ws_ref, wo_ref, fw_ref, o_ref):
    pr = jnp.dot(yr_ref[...], wr_ref[...], preferred_element_type=F32)
    ps = jnp.dot(ys_ref[...], ws_ref[...], preferred_element_type=F32)
    h = _sigmoid(gr_ref[...].astype(F32)) * pr + _sigmoid(gs_ref[...].astype(F32)) * ps
    o = x_ref[...] + jnp.dot(h.astype(BF16), wo_ref[...], preferred_element_type=F32)
    if final:
        o = o * lax.rsqrt(jnp.mean(o * o, axis=-1, keepdims=True) + RMS_EPS) * fw_ref[...]
    o_ref[...] = o


def _merge(y_rwkv, y_ssm, proj, x2, wr, ws, wo, fw, final, tm):
    t, d = x2.shape
    const = lambda i: (0, 0)
    return pl.pallas_call(
        functools.partial(_merge_kernel, final),
        grid=(t // tm,),
        in_specs=[pl.BlockSpec((tm, RWKV_WIDTH), lambda i: (i, 0)),
                  pl.BlockSpec((tm, SSM_WIDTH), lambda i: (i, 0)),
                  pl.BlockSpec((tm, d), lambda i: (i, OFF_G_RWKV // d)),
                  pl.BlockSpec((tm, d), lambda i: (i, OFF_G_SSM // d)),
                  pl.BlockSpec((tm, d), lambda i: (i, 0)),
                  pl.BlockSpec((RWKV_WIDTH, d), const),
                  pl.BlockSpec((SSM_WIDTH, d), const),
                  pl.BlockSpec((d, d), const),
                  pl.BlockSpec((1, d), const)],
        out_specs=pl.BlockSpec((tm, d), lambda i: (i, 0)),
        out_shape=jax.ShapeDtypeStruct((t, d), F32),
        compiler_params=pltpu.CompilerParams(
            dimension_semantics=("arbitrary",), vmem_limit_bytes=VMEM_LIMIT),
        name="merge",
    )(y_rwkv, y_ssm, proj, proj, x2, wr, ws, wo, fw)


def _pad_cols(w, width):
    return jnp.pad(w, ((0, 0), (0, width - w.shape[1])))


def _pad_rows(w, rows):
    return jnp.pad(w, ((0, rows - w.shape[0]), (0, 0)))


def _proj_weight(w_in, w_vres):
    d = w_in.shape[0]
    c = lambda a, b: w_in[:, a:b]
    rw = 0
    r, k, v = c(rw, rw + 1024), c(rw + 1024, rw + 2048), c(rw + 2048, rw + 3072)
    lora = c(rw + 3072, rw + 3200)
    z_rwkv = c(rw + 3200, rw + 4224)
    sm = 4224
    ssm_z = c(sm, sm + 2048)
    xs = c(sm + 2048, sm + 4096)
    bm = c(sm + 4096, sm + 4608)
    cm = c(sm + 4608, sm + 5120)
    dt = c(sm + 5120, sm + 5152)
    gt = sm + 5152
    g_rwkv, g_ssm = c(gt, gt + 1024), c(gt + 1024, gt + 2048)
    vres = jnp.zeros((d, LANES), w_in.dtype) if w_vres is None else _pad_cols(w_vres, LANES)
    dt4 = jnp.pad(dt.reshape(d, SSM_GROUPS, SSM_GROUP_HEADS),
                  ((0, 0), (0, 0), (0, LANES - SSM_GROUP_HEADS))).reshape(d, SSM_GROUPS * LANES)
    w = jnp.concatenate([ssm_z, g_rwkv, g_ssm, r, k, v, z_rwkv, xs, bm, cm, lora, vres, dt4], axis=1)
    return _pad_cols(w, PROJ_W).astype(BF16)


def _group_lanes(vec):
    return jnp.pad(vec.reshape(SSM_GROUPS, SSM_GROUP_HEADS),
                   ((0, 0), (0, LANES - SSM_GROUP_HEADS))).reshape(1, SSM_GROUPS * LANES)


def kernel(x, norm_w, w_in, w_in_vres, mu_rwkv, mu_vres, decay_w0, decay_w2, iclr_a0, iclr_a2, vres_v0, vres_v2, k_k, k_a, r_k, gn_w, gn_b, w_out_rwkv, conv_w, conv_b, dt_bias, a_log, d_skip, ssm_norm_w, w_out_ssm, w_out, final_norm_w):
    bsz, seq, d = x.shape
    depth = norm_w.shape[0]
    t = bsz * seq
    x2 = x.reshape(t, d)
    tm_proj = min(1024, t)
    tn_proj = 3072
    tl_rwkv = min(512, seq)
    tl_ssd = min(512, seq)
    tm_merge = min(512, t)
    proj_dtype = BF16

    v_first = None
    for i in range(depth):
        w = _proj_weight(w_in[i], None if i == 0 else w_in_vres[i - 1])
        proj = _inproj(x2, norm_w[i].reshape(1, d), w, tm_proj, tn_proj, proj_dtype)

        mu = mu_rwkv[i]
        rows = [mu[0:1024], mu[1024:2048], mu[2048:3072], mu[3200:4224], decay_w0[i], iclr_a0[i],
                k_k[i], k_a[i], r_k[i].reshape(-1), gn_w[i], gn_b[i],
                vres_v0[i - 1] if i > 0 else jnp.zeros((RWKV_WIDTH,), F32)]
        pvec = _pad_rows(jnp.stack(rows, axis=0), 16)
        mu_small = jnp.stack([mu[3072:3200],
                              jnp.pad(mu_vres[i - 1], (0, LANES - VRES_LORA)) if i > 0
                              else jnp.zeros((LANES,), F32)], axis=0)
        mu_small = _pad_rows(mu_small, SUBLANES)
        w2p = _pad_rows(decay_w2[i], LANES).astype(BF16)
        a2p = jnp.concatenate([jnp.zeros_like(iclr_a2[i]), iclr_a2[i]], axis=0).astype(BF16)
        v2p = _pad_rows(vres_v2[i - 1], LANES).astype(BF16) if i > 0 else None
        y_rwkv, v_first = _rwkv(proj, v_first, pvec, mu_small, w2p, a2p, v2p, bsz, seq, tl_rwkv)

        cw, cb = conv_w[i], conv_b[i]
        pwide = jnp.concatenate([cw[:, :SSM_WIDTH], cb[None, :SSM_WIDTH],
                                 jnp.repeat(d_skip[i], SSM_HEAD_DIM)[None, :],
                                 ssm_norm_w[i][None, :]], axis=0)
        pwide = _pad_rows(pwide, 16)
        nb = SSM_GROUPS * SSM_STATE
        psmall = jnp.concatenate([cw[:, SSM_WIDTH:SSM_WIDTH + nb], cb[None, SSM_WIDTH:SSM_WIDTH + nb],
                                  cw[:, SSM_WIDTH + nb:], cb[None, SSM_WIDTH + nb:],
                                  _group_lanes(dt_bias[i]), _group_lanes(a_log[i])], axis=0)
        psmall = _pad_rows(psmall, 16)
        y_ssm = _ssd(proj, pwide, psmall, bsz, seq, tl_ssd)

        final = i == depth - 1
        x2 = _merge(y_rwkv, y_ssm, proj, x2, w_out_rwkv[i].astype(BF16), w_out_ssm[i].astype(BF16),
                    w_out[i].astype(BF16), final_norm_w.reshape(1, d), final, tm_merge)
    return x2.reshape(bsz, seq, d)
```

```python
import functools

import jax
import jax.numpy as jnp
from jax import lax
from jax.experimental import pallas as pl
from jax.experimental.pallas import tpu as pltpu

F32 = jnp.float32
BF16 = jnp.bfloat16

D_MODEL = 1024
RWKV_HEADS = 16
RWKV_HEAD_DIM = 64
RWKV_WIDTH = RWKV_HEADS * RWKV_HEAD_DIM
DECAY_LORA = 64
ICLR_LORA = 64
VRES_LORA = 32
RWKV_GN_EPS = 64e-5
SSM_WIDTH = 2 * D_MODEL
SSM_HEAD_DIM = 64
SSM_HEADS = SSM_WIDTH // SSM_HEAD_DIM
SSM_GROUPS = 4
SSM_STATE = 128
CONV_WIDTH = 4
SSD_CHUNK = 128
RMS_EPS = 1e-5

LANES = 128
SUBLANES = 8
DECAY_SCALE = 0.6065306597126334
RWKV_CHUNK = 64
HEAD_GROUP = 4
GROUP_W = HEAD_GROUP * RWKV_HEAD_DIM
SSM_GROUP_W = SSM_WIDTH // SSM_GROUPS
SSM_GROUP_HEADS = SSM_HEADS // SSM_GROUPS

OFF_SSM_Z = 0
OFF_G_RWKV = 2048
OFF_G_SSM = 3072
OFF_R = 4096
OFF_K = 5120
OFF_V = 6144
OFF_Z = 7168
OFF_XS = 8192
OFF_B = 10240
OFF_C = 10752
OFF_LORA = 11264
OFF_VRES = 11392
OFF_DT = 11520
PROJ_W = 12288

VMEM_LIMIT = 56 * 1024 * 1024


def _sigmoid(x):
    return 0.5 * jnp.tanh(0.5 * x) + 0.5


def _softplus(x):
    return jnp.maximum(x, 0.0) + jnp.log(1.0 + jnp.exp(-jnp.abs(x)))


def _dot(a, b):
    return jnp.dot(a.astype(BF16), b.astype(BF16), preferred_element_type=F32)


def _dot_nt(a, b):
    return lax.dot_general(a.astype(BF16), b.astype(BF16), (((1,), (1,)), ((), ())),
                           preferred_element_type=F32)


def _dot_tn(a, b):
    return lax.dot_general(a.astype(BF16), b.astype(BF16), (((0,), (0,)), ((), ())),
                           preferred_element_type=F32)


def _split(x, passes):
    parts = []
    for _ in range(passes - 1):
        p = x.astype(BF16)
        parts.append(p)
        x = x - p.astype(F32)
    parts.append(x.astype(BF16))
    return parts


def _dot_right01(x, w01, passes):
    return sum(jnp.dot(p, w01, preferred_element_type=F32) for p in _split(x, passes))


def _dot_left01(w01, x, passes):
    return sum(jnp.dot(w01, p, preferred_element_type=F32) for p in _split(x, passes))


def _iota(shape, axis):
    return lax.broadcasted_iota(jnp.int32, shape, axis)


def _delayed_rows(ext_ref, u, delays, advance=None):
    tl = u.shape[0]
    ext_ref[SUBLANES:, :] = u
    out = [ext_ref[SUBLANES - s:SUBLANES - s + tl, :] for s in delays]
    last = u[tl - SUBLANES:tl, :]
    ext_ref[0:SUBLANES, :] = last if advance is None else jnp.where(advance, last, ext_ref[0:SUBLANES, :])
    return out


def _interleave(rounds, *gens_and_paces):
    done = [0.0] * len(gens_and_paces)
    for _ in range(rounds):
        for j, (g, pace) in enumerate(gens_and_paces):
            done[j] += pace
            while done[j] >= 1.0:
                done[j] -= 1.0
                next(g, None)
    for g, _ in gens_and_paces:
        for _ in g:
            pass


def _inproj_kernel(x_ref, nw_ref, w_ref, o_ref, xn_ref):
    @pl.when(pl.program_id(1) == 0)
    def _():
        x = x_ref[...]
        ms = jnp.mean(x * x, axis=-1, keepdims=True)
        xn_ref[...] = (x * lax.rsqrt(ms + RMS_EPS) * nw_ref[...]).astype(BF16)

    o_ref[...] = jnp.dot(xn_ref[...], w_ref[...], preferred_element_type=F32).astype(o_ref.dtype)


def _inproj(x2, nw, w, tm, tn, out_dtype):
    t, d = x2.shape
    npad = w.shape[1]
    return pl.pallas_call(
        _inproj_kernel,
        grid=(t // tm, npad // tn),
        in_specs=[
            pl.BlockSpec((tm, d), lambda i, j: (i, 0)),
            pl.BlockSpec((1, d), lambda i, j: (0, 0)),
            pl.BlockSpec((d, tn), lambda i, j: (0, j)),
        ],
        out_specs=pl.BlockSpec((tm, tn), lambda i, j: (i, j)),
        out_shape=jax.ShapeDtypeStruct((t, npad), out_dtype),
        scratch_shapes=[pltpu.VMEM((tm, d), BF16)],
        compiler_params=pltpu.CompilerParams(
            dimension_semantics=("arbitrary", "arbitrary"), vmem_limit_bytes=VMEM_LIMIT),
        name="inproj",
    )(x2, nw, w)


def _block_diag(x, lane_head):
    return jnp.concatenate(
        [jnp.where(lane_head == h, x, 0.0) for h in range(HEAD_GROUP)], axis=0)


def _rwkv_chunk_pre(chunks, out):
    L = RWKV_CHUNK
    W = GROUP_W
    row = _iota((L, W), 0)
    lane = _iota((L, W), 1)
    lane_head = lane >> 6
    lane_in = lane & (RWKV_HEAD_DIM - 1)
    incl = lane_in <= row
    strict = lane_in < row
    eye = jnp.where(lane_in == row, 1.0, 0.0).astype(F32)
    tri = jnp.where(_iota((L, L), 1) <= _iota((L, L), 0), 1.0, 0.0).astype(BF16)

    bd = functools.partial(_block_diag, lane_head=lane_head)
    n = range(len(chunks))
    r, k, v, ld, av, bv = ([c[i] for c in chunks] for i in range(6))

    cum = [_dot_left01(tri, ld[i], 2) for i in n]
    g_last = [jnp.exp(cum[i][L - 1:L, :]) for i in n]
    r_t = [r[i] * jnp.exp(cum[i]) for i in n]
    a_t = [av[i] * jnp.exp(cum[i] - ld[i]) for i in n]
    g_inv = [jnp.exp(-cum[i]) for i in n]
    b_t = [bv[i] * g_inv[i] for i in n]
    k_t = [k[i] * g_inv[i] for i in n]
    g_tail = [g_last[i] * g_inv[i] for i in n]
    b_e = [bv[i] * g_tail[i] for i in n]
    k_e = [k[i] * g_tail[i] for i in n]
    yield

    prod = [_dot_nt(jnp.concatenate([a_t[i], r_t[i]], axis=0),
                    jnp.concatenate([bd(b_t[i]), bd(k_t[i])], axis=0)) for i in n]
    m_ab = [jnp.where(strict, prod[i][:L, :W], 0.0) for i in n]
    m_ak = [jnp.where(strict, prod[i][:L, W:], 0.0) for i in n]
    p_rb = [jnp.where(incl, prod[i][L:, :W], 0.0) for i in n]
    p_rk = [jnp.where(incl, prod[i][L:, W:], 0.0) for i in n]
    yield

    x = [_dot(m_ab[i], bd(m_ab[i])) for i in n]
    t_inv = [eye + m_ab[i] for i in n]
    mp_v = [_dot(jnp.concatenate([m_ak[i], p_rk[i]], axis=0), bd(v[i])) for i in n]
    yield
    for _ in range(4):
        q = [_dot(jnp.concatenate([x[i], t_inv[i]], axis=0), bd(x[i])) for i in n]
        x = [q[i][:L] for i in n]
        t_inv = [t_inv[i] + q[i][L:] for i in n]
        yield
    t_inv = [t_inv[i] + _dot(t_inv[i], bd(x[i])) for i in n]
    yield

    au = [_dot(t_inv[i], jnp.concatenate([bd(a_t[i]), bd(mp_v[i][:L])], axis=1)) for i in n]
    yield
    ry = [_dot(p_rb[i], jnp.concatenate([bd(au[i][:, :W]), bd(au[i][:, W:])], axis=1)) for i in n]
    r_hat = [r_t[i] + ry[i][:, :W] for i in n]
    y0 = [ry[i][:, W:] + mp_v[i][L:] for i in n]
    out.extend(dict(ra=jnp.concatenate([r_hat[i], au[i][:, :W]], axis=0), y0=y0[i], u0=au[i][:, W:],
                    v=v[i], bk_e=jnp.concatenate([b_e[i], k_e[i]], axis=0),
                    g_last=g_last[i]) for i in n)


def _rwkv_chunk_state(s, c, same_head):
    L = RWKV_CHUNK
    ra = _dot_nt(c["ra"], s)
    yield None
    y = ra[:L] + c["y0"]
    uv = jnp.concatenate([(ra[L:] + c["u0"]).astype(BF16), c["v"].astype(BF16)], axis=0)
    s = s * c["g_last"] + jnp.where(same_head, _dot_tn(uv, c["bk_e"]), 0.0)
    yield y, s


def _rwkv_kernel(has_vres, n_chunks, nt, *refs):
    it = iter(refs)
    r_ref, k_ref, v_ref, z_ref, lora_ref = (next(it) for _ in range(5))
    vres_ref = next(it) if has_vres else None
    vf_ref = next(it) if has_vres else None
    pv_ref, pvp_ref, mul_ref, w2_ref, a2_ref = (next(it) for _ in range(5))
    v2_ref = next(it) if has_vres else None
    y_ref = next(it)
    vout_ref = None if has_vres else next(it)
    s_ref, cr, ck, cv, cz, cl, cvr = (next(it) for _ in range(7))
    st_ra, st_y0, st_u0, st_v, st_bke, st_gl, st_bon, st_zg = (next(it) for _ in range(8))
    stash = (st_ra, st_y0, st_u0, st_v, st_bke, st_gl, st_bon, st_zg)

    n = pl.program_id(0)
    last = pl.num_programs(0) - 2
    t_cur = jnp.minimum(n, last) % nt
    t_prev = jnp.maximum(n - 1, 0) % nt
    wslot = n % 2
    rslot = 1 - wslot
    advance = n < last

    @pl.when(n == 0)
    def _():
        s_ref[...] = jnp.zeros_like(s_ref)
        for st in stash:
            st[1] = jnp.zeros(st.shape[1:], st.dtype)

    @pl.when(t_cur == 0)
    def _():
        for c in (cr, ck, cv, cz, cl, cvr):
            c[0:SUBLANES, :] = jnp.zeros((SUBLANES, c.shape[1]), c.dtype)

    pv = pv_ref[...]
    prm = lambda i: pv[i:i + 1, :]
    mu_r, mu_k, mu_v, mu_z, w0, a0, k_k, k_a, r_k, _, _, v0 = (prm(i) for i in range(12))
    gn_w, gn_b = pvp_ref[9:10, :], pvp_ref[10:11, :]

    row = _iota((GROUP_W, GROUP_W), 0)
    lane = _iota((GROUP_W, GROUP_W), 1)
    same_head = (row >> 6) == (lane >> 6)
    seg_ones = jnp.where(same_head, 1.0, 0.0).astype(BF16)
    head_sum = lambda x: _dot_right01(x, seg_ones, 2)
    L = RWKV_CHUNK

    def shifted(ref, ext, mu):
        u = ref[...].astype(F32)
        prev, = _delayed_rows(ext, u, (1,), advance)
        return u + (prev - u) * mu

    def current_block():
        r = shifted(r_ref, cr, mu_r)
        k = shifted(k_ref, ck, mu_k)
        yield
        v = shifted(v_ref, cv, mu_v)
        lora = shifted(lora_ref, cl, mul_ref[0:1, :])
        yield
        ld = -DECAY_SCALE * _sigmoid(w0 + _dot(jnp.tanh(lora), w2_ref[...]))
        a = _sigmoid(a0 + _dot(lora, a2_ref[...]))
        yield
        kk = k * k_k
        kk = kk * lax.rsqrt(jnp.maximum(head_sum(kk * kk), 1e-24))
        k = k * (1.0 + (a - 1.0) * k_a)
        yield
        if has_vres:
            vres = shifted(vres_ref, cvr, mul_ref[1:2, :])
            v = v + (vf_ref[...] - v) * _sigmoid(v0 + _dot(vres, v2_ref[...]))
        else:
            vout_ref[...] = v
        yield
        z = shifted(z_ref, cz, mu_z)
        st_zg[wslot] = z * _sigmoid(z)
        st_bon[wslot] = head_sum(r * k * r_k) * v
        yield
        av = -kk
        bv = kk * a
        ins = [[x[c * L:(c + 1) * L, :] for x in (r, k, v, ld, av, bv)] for c in range(n_chunks)]
        out = []
        yield from _rwkv_chunk_pre(ins, out)
        for c, o in enumerate(out):
            st_ra[wslot, 2 * c * L:2 * (c + 1) * L, :] = o["ra"].astype(BF16)
            st_bke[wslot, 2 * c * L:2 * (c + 1) * L, :] = o["bk_e"].astype(BF16)
            st_y0[wslot, c * L:(c + 1) * L, :] = o["y0"]
            st_u0[wslot, c * L:(c + 1) * L, :] = o["u0"]
            st_v[wslot, c * L:(c + 1) * L, :] = o["v"].astype(BF16)
            st_gl[wslot, c * SUBLANES:(c + 1) * SUBLANES, :] = jnp.broadcast_to(
                o["g_last"], (SUBLANES, GROUP_W))

    def previous_block():
        s = jnp.where(t_prev == 0, 0.0, s_ref[...])
        ys = []
        for c in range(n_chunks):
            cd = dict(ra=st_ra[rslot, 2 * c * L:2 * (c + 1) * L, :],
                      bk_e=st_bke[rslot, 2 * c * L:2 * (c + 1) * L, :],
                      y0=st_y0[rslot, c * L:(c + 1) * L, :], u0=st_u0[rslot, c * L:(c + 1) * L, :],
                      v=st_v[rslot, c * L:(c + 1) * L, :],
                      g_last=st_gl[rslot, c * SUBLANES:c * SUBLANES + 1, :])
            for res in _rwkv_chunk_state(s, cd, same_head):
                yield
            y, s = res
            ys.append(y)
        s_ref[...] = s
        y = jnp.concatenate(ys, axis=0)
        inv_n = 1.0 / RWKV_HEAD_DIM
        mean = head_sum(y) * inv_n
        yc = y - mean
        yield
        var = head_sum(yc * yc) * inv_n
        y = yc * lax.rsqrt(var + RWKV_GN_EPS) * gn_w + gn_b
        y_ref[...] = ((y + st_bon[rslot]) * st_zg[rslot]).astype(y_ref.dtype)

    _interleave(20, (current_block(), 1.0), (previous_block(), 1.0))


def _rwkv(proj, v_first, pvec, mul, w2p, a2p, v2p, bsz, seq, tl):
    has_vres = v_first is not None
    nt = seq // tl
    t = bsz * seq
    gw = GROUP_W
    hg = RWKV_HEADS // HEAD_GROUP
    n_blocks = bsz * hg * nt
    cur = lambda n: jnp.minimum(n, n_blocks - 1)
    prv = lambda n: jnp.maximum(n - 1, 0)
    rows = lambda m: (m // (hg * nt)) * nt + m % nt
    grp = lambda m: (m // nt) % hg
    colblk = lambda off: (lambda n: (rows(cur(n)), off // gw + grp(cur(n))))
    smallblk = lambda off: (lambda n: (rows(cur(n)), off // LANES))
    in_specs = [pl.BlockSpec((tl, gw), colblk(OFF_R)),
                pl.BlockSpec((tl, gw), colblk(OFF_K)),
                pl.BlockSpec((tl, gw), colblk(OFF_V)),
                pl.BlockSpec((tl, gw), colblk(OFF_Z)),
                pl.BlockSpec((tl, LANES), smallblk(OFF_LORA))]
    args = [proj, proj, proj, proj, proj]
    if has_vres:
        in_specs += [pl.BlockSpec((tl, LANES), smallblk(OFF_VRES)),
                     pl.BlockSpec((tl, gw), colblk(0))]
        args += [proj, v_first]
    in_specs += [pl.BlockSpec((16, gw), lambda n: (0, grp(cur(n)))),
                 pl.BlockSpec((16, gw), lambda n: (0, grp(prv(n)))),
                 pl.BlockSpec((SUBLANES, LANES), lambda n: (0, 0)),
                 pl.BlockSpec((LANES, gw), lambda n: (0, grp(cur(n)))),
                 pl.BlockSpec((LANES, gw), lambda n: (0, grp(cur(n))))]
    args += [pvec, pvec, mul, w2p, a2p]
    if has_vres:
        in_specs += [pl.BlockSpec((LANES, gw), lambda n: (0, grp(cur(n))))]
        args += [v2p]
    out_shape = [jax.ShapeDtypeStruct((t, RWKV_WIDTH), BF16)]
    out_specs = [pl.BlockSpec((tl, gw), lambda n: (rows(prv(n)), grp(prv(n))))]
    if not has_vres:
        out_shape.append(jax.ShapeDtypeStruct((t, RWKV_WIDTH), F32))
        out_specs.append(pl.BlockSpec((tl, gw), colblk(0)))
    n_chunks = tl // RWKV_CHUNK
    stash = lambda rows, dtype: pltpu.VMEM((2, rows, gw), dtype)
    scratch = ([pltpu.VMEM((gw, gw), F32)]
               + [pltpu.VMEM((SUBLANES + tl, gw), F32)] * 4
               + [pltpu.VMEM((SUBLANES + tl, LANES), F32)] * 2
               + [stash(2 * tl, BF16), stash(tl, F32), stash(tl, F32), stash(tl, BF16),
                  stash(2 * tl, BF16), stash(SUBLANES * n_chunks, F32), stash(tl, F32), stash(tl, F32)])
    res = pl.pallas_call(
        functools.partial(_rwkv_kernel, has_vres, n_chunks, nt),
        grid=(n_blocks + 1,),
        in_specs=in_specs, out_specs=out_specs, out_shape=out_shape,
        scratch_shapes=scratch,
        compiler_params=pltpu.CompilerParams(
            dimension_semantics=("arbitrary",), vmem_limit_bytes=VMEM_LIMIT),
        name="rwkv",
    )(*args)
    return (res[0], v_first) if has_vres else (res[0], res[1])


def _ssd_chunk_pre(xs, bm, cm, dt, a_neg):
    L = SSD_CHUNK
    gw = SSM_GROUP_W
    tri = jnp.where(_iota((L, L), 1) <= _iota((L, L), 0), 1.0, 0.0).astype(BF16)
    causal = _iota((L, L), 1) <= _iota((L, L), 0)
    expand = jnp.where(_iota((LANES, gw), 0) == (_iota((LANES, gw), 1) >> 6), 1.0, 0.0).astype(BF16)

    cum = _dot_left01(tri, dt * a_neg, 3)
    wide = _dot_right01(jnp.concatenate([dt, cum], axis=0), expand, 2)
    dt_e = wide[:L]
    cum_e = wide[L:]
    ecum_e = jnp.exp(cum_e)
    tail_e = jnp.exp(cum_e[L - 1:L, :] - cum_e)
    xdt = xs * dt_e

    cb = _dot_nt(cm, bm)
    cum_t = cum.T
    lane_lo = _iota((L, LANES), 1) < SSM_HEAD_DIM
    pieces = []
    for pair in range(SSM_GROUP_HEADS // 2):
        x_pair = xdt[:, pair * LANES:(pair + 1) * LANES]
        outs = []
        for hh in (2 * pair, 2 * pair + 1):
            seg = cum[:, hh:hh + 1] - cum_t[hh:hh + 1, :]
            w_h = cb * jnp.exp(jnp.where(causal, seg, -jnp.inf))
            outs.append(_dot(w_h, x_pair))
        pieces.append(jnp.where(lane_lo, outs[0], outs[1]))
    y_intra = jnp.concatenate(pieces, axis=1)

    return y_intra, ecum_e, _dot_tn(bm, xdt * tail_e)


def _ssd_kernel(n_chunks, z_ref, x_ref, b_ref, c_ref, dt_ref, pw_ref, ps_ref, y_ref,
                st_ref, cx, cb_c, cc_c):
    @pl.when(pl.program_id(2) == 0)
    def _():
        st_ref[...] = jnp.zeros_like(st_ref)
        for c in (cx, cb_c, cc_c):
            c[0:SUBLANES, :] = jnp.zeros((SUBLANES, c.shape[1]), c.dtype)

    pw = pw_ref[...]
    ps = ps_ref[...]

    def conv_silu(ref, ext, taps, bias):
        u = ref[...].astype(F32)
        acc = u * taps[CONV_WIDTH - 1] + bias
        delays = tuple(range(1, CONV_WIDTH))
        for s, us in zip(delays, _delayed_rows(ext, u, delays)):
            acc = acc + us * taps[CONV_WIDTH - 1 - s]
        return acc * _sigmoid(acc)

    xs = conv_silu(x_ref, cx, [pw[i:i + 1, :] for i in range(4)], pw[4:5, :])
    bm = conv_silu(b_ref, cb_c, [ps[i:i + 1, :] for i in range(4)], ps[4:5, :])
    cm = conv_silu(c_ref, cc_c, [ps[i:i + 1, :] for i in range(5, 9)], ps[9:10, :])
    dt = _softplus(dt_ref[...].astype(F32) + ps[10:11, :])
    a_neg = -jnp.exp(ps[11:12, :])

    L = SSD_CHUNK
    sl = lambda t, c: t[c * L:(c + 1) * L, :]
    pre = [_ssd_chunk_pre(sl(xs, c), sl(bm, c), sl(cm, c), sl(dt, c), a_neg) for c in range(n_chunks)]
    st = st_ref[...]
    ys = []
    for c, (y_intra, ecum_e, upd) in enumerate(pre):
        ys.append(y_intra + _dot(sl(cm, c), st) * ecum_e)
        st = st * ecum_e[L - 1:L, :] + upd
    st_ref[...] = st

    z = z_ref[...].astype(F32)
    y = (jnp.concatenate(ys, axis=0) + pw[5:6, :] * xs) * (z * _sigmoid(z))
    y = y * lax.rsqrt(jnp.mean(y * y, axis=-1, keepdims=True) + RMS_EPS)
    y_ref[...] = (y * pw[6:7, :]).astype(y_ref.dtype)


def _ssd(proj, pwide, psmall, bsz, seq, tl):
    nt = seq // tl
    t = bsz * seq
    gw = SSM_GROUP_W
    blk = lambda off, w: (lambda b, g, i: (b * nt + i, off // w + g))
    return pl.pallas_call(
        functools.partial(_ssd_kernel, tl // SSD_CHUNK),
        grid=(bsz, SSM_GROUPS, nt),
        in_specs=[pl.BlockSpec((tl, gw), blk(OFF_SSM_Z, gw)),
                  pl.BlockSpec((tl, gw), blk(OFF_XS, gw)),
                  pl.BlockSpec((tl, LANES), blk(OFF_B, LANES)),
                  pl.BlockSpec((tl, LANES), blk(OFF_C, LANES)),
                  pl.BlockSpec((tl, LANES), blk(OFF_DT, LANES)),
                  pl.BlockSpec((16, gw), lambda b, g, i: (0, g)),
                  pl.BlockSpec((16, LANES), lambda b, g, i: (0, g))],
        out_specs=pl.BlockSpec((tl, gw), lambda b, g, i: (b * nt + i, g)),
        out_shape=jax.ShapeDtypeStruct((t, SSM_WIDTH), BF16),
        scratch_shapes=[pltpu.VMEM((SSM_STATE, gw), F32),
                        pltpu.VMEM((SUBLANES + tl, gw), F32),
                        pltpu.VMEM((SUBLANES + tl, LANES), F32),
                        pltpu.VMEM((SUBLANES + tl, LANES), F32)],
        compiler_params=pltpu.CompilerParams(
            dimension_semantics=("arbitrary", "arbitrary", "arbitrary"),
            vmem_limit_bytes=VMEM_LIMIT),
        name="ssd",
    )(proj, proj, proj, proj, proj, pwide, psmall)


def _merge_kernel(final, yr_ref, ys_ref, gr_ref, gs_ref, x_ref, wr_ref, ws_ref, wo_ref, fw_ref, o_ref):
    pr = jnp.dot(yr_ref[...], wr_ref[...], preferred_element_type=F32)
    ps = jnp.dot(ys_ref[...], ws_ref[...], preferred_element_type=F32)
    h = _sigmoid(gr_ref[...].astype(F32)) * pr + _sigmoid(gs_ref[...].astype(F32)) * ps
    o = x_ref[...] + jnp.dot(h.astype(BF16), wo_ref[...], preferred_element_type=F32)
    if final:
        o = o * lax.rsqrt(jnp.mean(o * o, axis=-1, keepdims=True) + RMS_EPS) * fw_ref[...]
    o_ref[...] = o


def _merge(y_rwkv, y_ssm, proj, x2, wr, ws, wo, fw, final, tm):
    t, d = x2.shape
    const = lambda i: (0, 0)
    return pl.pallas_call(
        functools.partial(_merge_kernel, final),
        grid=(t // tm,),
        in_specs=[pl.BlockSpec((tm, RWKV_WIDTH), lambda i: (i, 0)),
                  pl.BlockSpec((tm, SSM_WIDTH), lambda i: (i, 0)),
                  pl.BlockSpec((tm, d), lambda i: (i, OFF_G_RWKV // d)),
                  pl.BlockSpec((tm, d), lambda i: (i, OFF_G_SSM // d)),
                  pl.BlockSpec((tm, d), lambda i: (i, 0)),
                  pl.BlockSpec((RWKV_WIDTH, d), const),
                  pl.BlockSpec((SSM_WIDTH, d), const),
                  pl.BlockSpec((d, d), const),
                  pl.BlockSpec((1, d), const)],
        out_specs=pl.BlockSpec((tm, d), lambda i: (i, 0)),
        out_shape=jax.ShapeDtypeStruct((t, d), F32),
        compiler_params=pltpu.CompilerParams(
            dimension_semantics=("arbitrary",), vmem_limit_bytes=VMEM_LIMIT),
        name="merge",
    )(y_rwkv, y_ssm, proj, proj, x2, wr, ws, wo, fw)


def _pad_cols(w, width):
    return jnp.pad(w, ((0, 0), (0, width - w.shape[1])))


def _pad_rows(w, rows):
    return jnp.pad(w, ((0, rows - w.shape[0]), (0, 0)))


def _proj_weight(w_in, w_vres):
    d = w_in.shape[0]
    c = lambda a, b: w_in[:, a:b]
    rw = 0
    r, k, v = c(rw, rw + 1024), c(rw + 1024, rw + 2048), c(rw + 2048, rw + 3072)
    lora = c(rw + 3072, rw + 3200)
    z_rwkv = c(rw + 3200, rw + 4224)
    sm = 4224
    ssm_z = c(sm, sm + 2048)
    xs = c(sm + 2048, sm + 4096)
    bm = c(sm + 4096, sm + 4608)
    cm = c(sm + 4608, sm + 5120)
    dt = c(sm + 5120, sm + 5152)
    gt = sm + 5152
    g_rwkv, g_ssm = c(gt, gt + 1024), c(gt + 1024, gt + 2048)
    vres = jnp.zeros((d, LANES), w_in.dtype) if w_vres is None else _pad_cols(w_vres, LANES)
    dt4 = jnp.pad(dt.reshape(d, SSM_GROUPS, SSM_GROUP_HEADS),
                  ((0, 0), (0, 0), (0, LANES - SSM_GROUP_HEADS))).reshape(d, SSM_GROUPS * LANES)
    w = jnp.concatenate([ssm_z, g_rwkv, g_ssm, r, k, v, z_rwkv, xs, bm, cm, lora, vres, dt4], axis=1)
    return _pad_cols(w, PROJ_W).astype(BF16)


def _group_lanes(vec):
    return jnp.pad(vec.reshape(SSM_GROUPS, SSM_GROUP_HEADS),
                   ((0, 0), (0, LANES - SSM_GROUP_HEADS))).reshape(1, SSM_GROUPS * LANES)


def kernel(x, norm_w, w_in, w_in_vres, mu_rwkv, mu_vres, decay_w0, decay_w2, iclr_a0, iclr_a2, vres_v0, vres_v2, k_k, k_a, r_k, gn_w, gn_b, w_out_rwkv, conv_w, conv_b, dt_bias, a_log, d_skip, ssm_norm_w, w_out_ssm, w_out, final_norm_w):
    bsz, seq, d = x.shape
    depth = norm_w.shape[0]
    t = bsz * seq
    x2 = x.reshape(t, d)
    tm_proj = min(1024, t)
    tn_proj = 3072
    tl_rwkv = min(512, seq)
    tl_ssd = min(1024, seq)
    tm_merge = min(512, t)
    proj_dtype = BF16

    v_first = None
    for i in range(depth):
        w = _proj_weight(w_in[i], None if i == 0 else w_in_vres[i - 1])
        proj = _inproj(x2, norm_w[i].reshape(1, d), w, tm_proj, tn_proj, proj_dtype)

        mu = mu_rwkv[i]
        rows = [mu[0:1024], mu[1024:2048], mu[2048:3072], mu[3200:4224], decay_w0[i], iclr_a0[i],
                k_k[i], k_a[i], r_k[i].reshape(-1), gn_w[i], gn_b[i],
                vres_v0[i - 1] if i > 0 else jnp.zeros((RWKV_WIDTH,), F32)]
        pvec = _pad_rows(jnp.stack(rows, axis=0), 16)
        mu_small = jnp.stack([mu[3072:3200],
                              jnp.pad(mu_vres[i - 1], (0, LANES - VRES_LORA)) if i > 0
                              else jnp.zeros((LANES,), F32)], axis=0)
        mu_small = _pad_rows(mu_small, SUBLANES)
        w2p = _pad_rows(decay_w2[i], LANES).astype(BF16)
        a2p = jnp.concatenate([jnp.zeros_like(iclr_a2[i]), iclr_a2[i]], axis=0).astype(BF16)
        v2p = _pad_rows(vres_v2[i - 1], LANES).astype(BF16) if i > 0 else None
        y_rwkv, v_first = _rwkv(proj, v_first, pvec, mu_small, w2p, a2p, v2p, bsz, seq, tl_rwkv)

        cw, cb = conv_w[i], conv_b[i]
        pwide = jnp.concatenate([cw[:, :SSM_WIDTH], cb[None, :SSM_WIDTH],
                                 jnp.repeat(d_skip[i], SSM_HEAD_DIM)[None, :],
                                 ssm_norm_w[i][None, :]], axis=0)
        pwide = _pad_rows(pwide, 16)
        nb = SSM_GROUPS * SSM_STATE
        psmall = jnp.concatenate([cw[:, SSM_WIDTH:SSM_WIDTH + nb], cb[None, SSM_WIDTH:SSM_WIDTH + nb],
                                  cw[:, SSM_WIDTH + nb:], cb[None, SSM_WIDTH + nb:],
                                  _group_lanes(dt_bias[i]), _group_lanes(a_log[i])], axis=0)
        psmall = _pad_rows(psmall, 16)
        y_ssm = _ssd(proj, pwide, psmall, bsz, seq, tl_ssd)

        final = i == depth - 1
        x2 = _merge(y_rwkv, y_ssm, proj, x2, w_out_rwkv[i].astype(BF16), w_out_ssm[i].astype(BF16),
                    w_out[i].astype(BF16), final_norm_w.reshape(1, d), final, tm_merge)
    return x2.reshape(bsz, seq, d)
```

```python
import functools

import jax
import jax.numpy as jnp
from jax import lax
from jax.experimental import pallas as pl
from jax.experimental.pallas import tpu as pltpu

F32 = jnp.float32
BF16 = jnp.bfloat16

D_MODEL = 1024
RWKV_HEADS = 16
RWKV_HEAD_DIM = 64
RWKV_WIDTH = RWKV_HEADS * RWKV_HEAD_DIM
DECAY_LORA = 64
ICLR_LORA = 64
VRES_LORA = 32
RWKV_GN_EPS = 64e-5
SSM_WIDTH = 2 * D_MODEL
SSM_HEAD_DIM = 64
SSM_HEADS = SSM_WIDTH // SSM_HEAD_DIM
SSM_GROUPS = 4
SSM_STATE = 128
CONV_WIDTH = 4
SSD_CHUNK = 128
RMS_EPS = 1e-5

LANES = 128
SUBLANES = 8
DECAY_SCALE = 0.6065306597126334
RWKV_CHUNK = 64
HEAD_GROUP = 4
GROUP_W = HEAD_GROUP * RWKV_HEAD_DIM
SSM_GROUP_W = SSM_WIDTH // SSM_GROUPS
SSM_GROUP_HEADS = SSM_HEADS // SSM_GROUPS

OFF_SSM_Z = 0
OFF_G_RWKV = 2048
OFF_G_SSM = 3072
OFF_R = 4096
OFF_K = 5120
OFF_V = 6144
OFF_Z = 7168
OFF_XS = 8192
OFF_B = 10240
OFF_C = 10752
OFF_LORA = 11264
OFF_VRES = 11392
OFF_DT = 11520
PROJ_W = 12288

VMEM_LIMIT = 56 * 1024 * 1024


def _sigmoid(x):
    return 0.5 * jnp.tanh(0.5 * x) + 0.5


def _softplus(x):
    return jnp.maximum(x, 0.0) + jnp.log(1.0 + jnp.exp(-jnp.abs(x)))


def _dot(a, b):
    return jnp.dot(a.astype(BF16), b.astype(BF16), preferred_element_type=F32)


def _dot_nt(a, b):
    return lax.dot_general(a.astype(BF16), b.astype(BF16), (((1,), (1,)), ((), ())),
                           preferred_element_type=F32)


def _dot_tn(a, b):
    return lax.dot_general(a.astype(BF16), b.astype(BF16), (((0,), (0,)), ((), ())),
                           preferred_element_type=F32)


def _split(x, passes):
    parts = []
    for _ in range(passes - 1):
        p = x.astype(BF16)
        parts.append(p)
        x = x - p.astype(F32)
    parts.append(x.astype(BF16))
    return parts


def _dot_right01(x, w01, passes):
    return sum(jnp.dot(p, w01, preferred_element_type=F32) for p in _split(x, passes))


def _dot_left01(w01, x, passes):
    return sum(jnp.dot(w01, p, preferred_element_type=F32) for p in _split(x, passes))


def _iota(shape, axis):
    return lax.broadcasted_iota(jnp.int32, shape, axis)


def _delayed_rows(ext_ref, u, delays, advance=None):
    tl = u.shape[0]
    ext_ref[SUBLANES:, :] = u
    out = [ext_ref[SUBLANES - s:SUBLANES - s + tl, :] for s in delays]
    last = u[tl - SUBLANES:tl, :]
    ext_ref[0:SUBLANES, :] = last if advance is None else jnp.where(advance, last, ext_ref[0:SUBLANES, :])
    return out


def _interleave(rounds, *gens_and_paces):
    done = [0.0] * len(gens_and_paces)
    for _ in range(rounds):
        for j, (g, pace) in enumerate(gens_and_paces):
            done[j] += pace
            while done[j] >= 1.0:
                done[j] -= 1.0
                next(g, None)
    for g, _ in gens_and_paces:
        for _ in g:
            pass


def _inproj_kernel(x_ref, nw_ref, w_ref, o_ref, xn_ref):
    @pl.when(pl.program_id(1) == 0)
    def _():
        x = x_ref[...]
        ms = jnp.mean(x * x, axis=-1, keepdims=True)
        xn_ref[...] = (x * lax.rsqrt(ms + RMS_EPS) * nw_ref[...]).astype(BF16)

    o_ref[...] = jnp.dot(xn_ref[...], w_ref[...], preferred_element_type=F32).astype(o_ref.dtype)


def _inproj(x2, nw, w, tm, tn, out_dtype):
    t, d = x2.shape
    npad = w.shape[1]
    return pl.pallas_call(
        _inproj_kernel,
        grid=(t // tm, npad // tn),
        in_specs=[
            pl.BlockSpec((tm, d), lambda i, j: (i, 0)),
            pl.BlockSpec((1, d), lambda i, j: (0, 0)),
            pl.BlockSpec((d, tn), lambda i, j: (0, j)),
        ],
        out_specs=pl.BlockSpec((tm, tn), lambda i, j: (i, j)),
        out_shape=jax.ShapeDtypeStruct((t, npad), out_dtype),
        scratch_shapes=[pltpu.VMEM((tm, d), BF16)],
        compiler_params=pltpu.CompilerParams(
            dimension_semantics=("arbitrary", "arbitrary"), vmem_limit_bytes=VMEM_LIMIT),
        name="inproj",
    )(x2, nw, w)


def _block_diag(x, lane_head):
    return jnp.concatenate(
        [jnp.where(lane_head == h, x, 0.0) for h in range(HEAD_GROUP)], axis=0)


def _rwkv_chunk_pre(chunks, out):
    L = RWKV_CHUNK
    W = GROUP_W
    row = _iota((L, W), 0)
    lane = _iota((L, W), 1)
    lane_head = lane >> 6
    lane_in = lane & (RWKV_HEAD_DIM - 1)
    incl = lane_in <= row
    strict = lane_in < row
    eye = jnp.where(lane_in == row, 1.0, 0.0).astype(F32)
    tri = jnp.where(_iota((L, L), 1) <= _iota((L, L), 0), 1.0, 0.0).astype(BF16)

    bd = functools.partial(_block_diag, lane_head=lane_head)
    n = range(len(chunks))
    r, k, v, ld, av, bv = ([c[i] for c in chunks] for i in range(6))

    cum = [_dot_left01(tri, ld[i], 2) for i in n]
    g_last = [jnp.exp(cum[i][L - 1:L, :]) for i in n]
    r_t = [r[i] * jnp.exp(cum[i]) for i in n]
    a_t = [av[i] * jnp.exp(cum[i] - ld[i]) for i in n]
    g_inv = [jnp.exp(-cum[i]) for i in n]
    b_t = [bv[i] * g_inv[i] for i in n]
    k_t = [k[i] * g_inv[i] for i in n]
    g_tail = [g_last[i] * g_inv[i] for i in n]
    b_e = [bv[i] * g_tail[i] for i in n]
    k_e = [k[i] * g_tail[i] for i in n]
    yield

    prod = [_dot_nt(jnp.concatenate([a_t[i], r_t[i]], axis=0),
                    jnp.concatenate([bd(b_t[i]), bd(k_t[i])], axis=0)) for i in n]
    m_ab = [jnp.where(strict, prod[i][:L, :W], 0.0) for i in n]
    m_ak = [jnp.where(strict, prod[i][:L, W:], 0.0) for i in n]
    p_rb = [jnp.where(incl, prod[i][L:, :W], 0.0) for i in n]
    p_rk = [jnp.where(incl, prod[i][L:, W:], 0.0) for i in n]
    yield

    x = [_dot(m_ab[i], bd(m_ab[i])) for i in n]
    t_inv = [eye + m_ab[i] for i in n]
    mp_v = [_dot(jnp.concatenate([m_ak[i], p_rk[i]], axis=0), bd(v[i])) for i in n]
    yield
    for _ in range(4):
        q = [_dot(jnp.concatenate([x[i], t_inv[i]], axis=0), bd(x[i])) for i in n]
        x = [q[i][:L] for i in n]
        t_inv = [t_inv[i] + q[i][L:] for i in n]
        yield
    t_inv = [t_inv[i] + _dot(t_inv[i], bd(x[i])) for i in n]
    yield

    au = [_dot(t_inv[i], jnp.concatenate([bd(a_t[i]), bd(mp_v[i][:L])], axis=1)) for i in n]
    yield
    ry = [_dot(p_rb[i], jnp.concatenate([bd(au[i][:, :W]), bd(au[i][:, W:])], axis=1)) for i in n]
    r_hat = [r_t[i] + ry[i][:, :W] for i in n]
    y0 = [ry[i][:, W:] + mp_v[i][L:] for i in n]
    out.extend(dict(ra=jnp.concatenate([r_hat[i], au[i][:, :W]], axis=0), y0=y0[i], u0=au[i][:, W:],
                    v=v[i], bk_e=jnp.concatenate([b_e[i], k_e[i]], axis=0),
                    g_last=g_last[i]) for i in n)


def _rwkv_chunk_state(s, c, same_head):
    L = RWKV_CHUNK
    ra = _dot_nt(c["ra"], s)
    yield None
    y = ra[:L] + c["y0"]
    uv = jnp.concatenate([(ra[L:] + c["u0"]).astype(BF16), c["v"].astype(BF16)], axis=0)
    s = s * c["g_last"] + jnp.where(same_head, _dot_tn(uv, c["bk_e"]), 0.0)
    yield y, s


def _rwkv_kernel(has_vres, n_chunks, nt, *refs):
    it = iter(refs)
    r_ref, k_ref, v_ref, z_ref, lora_ref = (next(it) for _ in range(5))
    vres_ref = next(it) if has_vres else None
    vf_ref = next(it) if has_vres else None
    pv_ref, pvp_ref, mul_ref, w2_ref, a2_ref = (next(it) for _ in range(5))
    v2_ref = next(it) if has_vres else None
    y_ref = next(it)
    vout_ref = None if has_vres else next(it)
    s_ref, cr, ck, cv, cz, cl, cvr = (next(it) for _ in range(7))
    st_ra, st_y0, st_u0, st_v, st_bke, st_gl, st_bon, st_zg = (next(it) for _ in range(8))
    stash = (st_ra, st_y0, st_u0, st_v, st_bke, st_gl, st_bon, st_zg)

    n = pl.program_id(0)
    last = pl.num_programs(0) - 2
    t_cur = jnp.minimum(n, last) % nt
    t_prev = jnp.maximum(n - 1, 0) % nt
    wslot = n % 2
    rslot = 1 - wslot
    advance = n < last

    @pl.when(n == 0)
    def _():
        s_ref[...] = jnp.zeros_like(s_ref)
        for st in stash:
            st[1] = jnp.zeros(st.shape[1:], st.dtype)

    @pl.when(t_cur == 0)
    def _():
        for c in (cr, ck, cv, cz, cl, cvr):
            c[0:SUBLANES, :] = jnp.zeros((SUBLANES, c.shape[1]), c.dtype)

    pv = pv_ref[...]
    prm = lambda i: pv[i:i + 1, :]
    mu_r, mu_k, mu_v, mu_z, w0, a0, k_k, k_a, r_k, _, _, v0 = (prm(i) for i in range(12))
    gn_w, gn_b = pvp_ref[9:10, :], pvp_ref[10:11, :]

    row = _iota((GROUP_W, GROUP_W), 0)
    lane = _iota((GROUP_W, GROUP_W), 1)
    same_head = (row >> 6) == (lane >> 6)
    seg_ones = jnp.where(same_head, 1.0, 0.0).astype(BF16)
    head_sum = lambda x, passes=2: _dot_right01(x, seg_ones, passes)
    L = RWKV_CHUNK

    tl = r_ref.shape[0]

    def current_block(r0, r1):
        def shifted(ref, ext, mu):
            u = ref[r0:r1, :].astype(F32)
            ext[SUBLANES + r0:SUBLANES + r1, :] = u
            prev = ext[SUBLANES - 1 + r0:SUBLANES - 1 + r1, :]
            if r1 == tl:
                ext[0:SUBLANES, :] = jnp.where(advance, u[r1 - r0 - SUBLANES:, :], ext[0:SUBLANES, :])
            return u + (prev - u) * mu

        c0, nc = r0 // L, (r1 - r0) // L
        r = shifted(r_ref, cr, mu_r)
        k = shifted(k_ref, ck, mu_k)
        yield
        v = shifted(v_ref, cv, mu_v)
        lora = shifted(lora_ref, cl, mul_ref[0:1, :])
        yield
        ld = -DECAY_SCALE * _sigmoid(w0 + _dot(jnp.tanh(lora), w2_ref[...]))
        a = _sigmoid(a0 + _dot(lora, a2_ref[...]))
        yield
        kk = k * k_k
        kk = kk * lax.rsqrt(jnp.maximum(head_sum(kk * kk, 1), 1e-24))
        k = k * (1.0 + (a - 1.0) * k_a)
        yield
        if has_vres:
            vres = shifted(vres_ref, cvr, mul_ref[1:2, :])
            v = v + (vf_ref[r0:r1, :] - v) * _sigmoid(v0 + _dot(vres, v2_ref[...]))
        else:
            vout_ref[r0:r1, :] = v
        yield
        z = shifted(z_ref, cz, mu_z)
        st_zg[wslot, r0:r1, :] = z * _sigmoid(z)
        st_bon[wslot, r0:r1, :] = head_sum(r * k * r_k) * v
        yield
        av = -kk
        bv = kk * a
        ins = [[x[c * L:(c + 1) * L, :] for x in (r, k, v, ld, av, bv)] for c in range(nc)]
        out = []
        yield from _rwkv_chunk_pre(ins, out)
        for c, o in enumerate(out, start=c0):
            st_ra[wslot, 2 * c * L:2 * (c + 1) * L, :] = o["ra"].astype(BF16)
            st_bke[wslot, 2 * c * L:2 * (c + 1) * L, :] = o["bk_e"].astype(BF16)
            st_y0[wslot, c * L:(c + 1) * L, :] = o["y0"]
            st_u0[wslot, c * L:(c + 1) * L, :] = o["u0"]
            st_v[wslot, c * L:(c + 1) * L, :] = o["v"].astype(BF16)
            st_gl[wslot, c * SUBLANES:(c + 1) * SUBLANES, :] = jnp.broadcast_to(
                o["g_last"], (SUBLANES, GROUP_W))

    def previous_block():
        s = jnp.where(t_prev == 0, 0.0, s_ref[...])
        ys = []
        for c in range(n_chunks):
            cd = dict(ra=st_ra[rslot, 2 * c * L:2 * (c + 1) * L, :],
                      bk_e=st_bke[rslot, 2 * c * L:2 * (c + 1) * L, :],
                      y0=st_y0[rslot, c * L:(c + 1) * L, :], u0=st_u0[rslot, c * L:(c + 1) * L, :],
                      v=st_v[rslot, c * L:(c + 1) * L, :],
                      g_last=st_gl[rslot, c * SUBLANES:c * SUBLANES + 1, :])
            for res in _rwkv_chunk_state(s, cd, same_head):
                yield
            y, s = res
            ys.append(y)
        s_ref[...] = s
        y = jnp.concatenate(ys, axis=0)
        inv_n = 1.0 / RWKV_HEAD_DIM
        mean = head_sum(y) * inv_n
        yc = y - mean
        yield
        var = head_sum(yc * yc, 1) * inv_n
        y = yc * lax.rsqrt(var + RWKV_GN_EPS) * gn_w + gn_b
        y_ref[...] = ((y + st_bon[rslot]) * st_zg[rslot]).astype(y_ref.dtype)

    _interleave(20, (current_block(0, tl), 1.0), (previous_block(), 1.0))


def _rwkv(proj, v_first, pvec, mul, w2p, a2p, v2p, bsz, seq, tl):
    has_vres = v_first is not None
    nt = seq // tl
    t = bsz * seq
    gw = GROUP_W
    hg = RWKV_HEADS // HEAD_GROUP
    n_blocks = bsz * hg * nt
    cur = lambda n: jnp.minimum(n, n_blocks - 1)
    prv = lambda n: jnp.maximum(n - 1, 0)
    rows = lambda m: (m // (hg * nt)) * nt + m % nt
    grp = lambda m: (m // nt) % hg
    colblk = lambda off: (lambda n: (rows(cur(n)), off // gw + grp(cur(n))))
    smallblk = lambda off: (lambda n: (rows(cur(n)), off // LANES))
    in_specs = [pl.BlockSpec((tl, gw), colblk(OFF_R)),
                pl.BlockSpec((tl, gw), colblk(OFF_K)),
                pl.BlockSpec((tl, gw), colblk(OFF_V)),
                pl.BlockSpec((tl, gw), colblk(OFF_Z)),
                pl.BlockSpec((tl, LANES), smallblk(OFF_LORA))]
    args = [proj, proj, proj, proj, proj]
    if has_vres:
        in_specs += [pl.BlockSpec((tl, LANES), smallblk(OFF_VRES)),
                     pl.BlockSpec((tl, gw), colblk(0))]
        args += [proj, v_first]
    in_specs += [pl.BlockSpec((16, gw), lambda n: (0, grp(cur(n)))),
                 pl.BlockSpec((16, gw), lambda n: (0, grp(prv(n)))),
                 pl.BlockSpec((SUBLANES, LANES), lambda n: (0, 0)),
                 pl.BlockSpec((LANES, gw), lambda n: (0, grp(cur(n)))),
                 pl.BlockSpec((LANES, gw), lambda n: (0, grp(cur(n))))]
    args += [pvec, pvec, mul, w2p, a2p]
    if has_vres:
        in_specs += [pl.BlockSpec((LANES, gw), lambda n: (0, grp(cur(n))))]
        args += [v2p]
    out_shape = [jax.ShapeDtypeStruct((t, RWKV_WIDTH), BF16)]
    out_specs = [pl.BlockSpec((tl, gw), lambda n: (rows(prv(n)), grp(prv(n))))]
    if not has_vres:
        out_shape.append(jax.ShapeDtypeStruct((t, RWKV_WIDTH), F32))
        out_specs.append(pl.BlockSpec((tl, gw), colblk(0)))
    n_chunks = tl // RWKV_CHUNK
    stash = lambda rows, dtype: pltpu.VMEM((2, rows, gw), dtype)
    scratch = ([pltpu.VMEM((gw, gw), F32)]
               + [pltpu.VMEM((SUBLANES + tl, gw), F32)] * 4
               + [pltpu.VMEM((SUBLANES + tl, LANES), F32)] * 2
               + [stash(2 * tl, BF16), stash(tl, F32), stash(tl, F32), stash(tl, BF16),
                  stash(2 * tl, BF16), stash(SUBLANES * n_chunks, F32), stash(tl, F32), stash(tl, F32)])
    res = pl.pallas_call(
        functools.partial(_rwkv_kernel, has_vres, n_chunks, nt),
        grid=(n_blocks + 1,),
        in_specs=in_specs, out_specs=out_specs, out_shape=out_shape,
        scratch_shapes=scratch,
        compiler_params=pltpu.CompilerParams(
            dimension_semantics=("arbitrary",), vmem_limit_bytes=VMEM_LIMIT),
        name="rwkv",
    )(*args)
    return (res[0], v_first) if has_vres else (res[0], res[1])


def _ssd_chunk_pre(xs, bm, cm, dt, a_neg):
    L = SSD_CHUNK
    gw = SSM_GROUP_W
    tri = jnp.where(_iota((L, L), 1) <= _iota((L, L), 0), 1.0, 0.0).astype(BF16)
    causal = _iota((L, L), 1) <= _iota((L, L), 0)
    expand = jnp.where(_iota((LANES, gw), 0) == (_iota((LANES, gw), 1) >> 6), 1.0, 0.0).astype(BF16)

    cum = _dot_left01(tri, dt * a_neg, 3)
    dt_e = _dot_right01(dt, expand, 2)
    cum_e = _dot_right01(cum, expand, 3)
    ecum_e = jnp.exp(cum_e)
    tail_e = jnp.exp(cum_e[L - 1:L, :] - cum_e)
    xdt = xs * dt_e

    cb = _dot_nt(cm, bm)
    cum_t = cum.T
    lane_lo = _iota((L, LANES), 1) < SSM_HEAD_DIM
    pieces = []
    for pair in range(SSM_GROUP_HEADS // 2):
        x_pair = xdt[:, pair * LANES:(pair + 1) * LANES]
        outs = []
        for hh in (2 * pair, 2 * pair + 1):
            seg = cum[:, hh:hh + 1] - cum_t[hh:hh + 1, :]
            w_h = cb * jnp.exp(jnp.where(causal, seg, -jnp.inf))
            outs.append(_dot(w_h, x_pair))
        pieces.append(jnp.where(lane_lo, outs[0], outs[1]))
    y_intra = jnp.concatenate(pieces, axis=1)

    return y_intra, ecum_e, _dot_tn(bm, xdt * tail_e)


def _ssd_kernel(n_chunks, z_ref, x_ref, b_ref, c_ref, dt_ref, pw_ref, ps_ref, y_ref,
                st_ref, cx, cb_c, cc_c):
    @pl.when(pl.program_id(2) == 0)
    def _():
        st_ref[...] = jnp.zeros_like(st_ref)
        for c in (cx, cb_c, cc_c):
            c[0:SUBLANES, :] = jnp.zeros((SUBLANES, c.shape[1]), c.dtype)

    pw = pw_ref[...]
    ps = ps_ref[...]

    def conv_silu(ref, ext, taps, bias):
        u = ref[...].astype(F32)
        acc = u * taps[CONV_WIDTH - 1] + bias
        delays = tuple(range(1, CONV_WIDTH))
        for s, us in zip(delays, _delayed_rows(ext, u, delays)):
            acc = acc + us * taps[CONV_WIDTH - 1 - s]
        return acc * _sigmoid(acc)

    xs = conv_silu(x_ref, cx, [pw[i:i + 1, :] for i in range(4)], pw[4:5, :])
    bm = conv_silu(b_ref, cb_c, [ps[i:i + 1, :] for i in range(4)], ps[4:5, :])
    cm = conv_silu(c_ref, cc_c, [ps[i:i + 1, :] for i in range(5, 9)], ps[9:10, :])
    dt = _softplus(dt_ref[...].astype(F32) + ps[10:11, :])
    a_neg = -jnp.exp(ps[11:12, :])

    L = SSD_CHUNK
    sl = lambda t, c: t[c * L:(c + 1) * L, :]
    pre = [_ssd_chunk_pre(sl(xs, c), sl(bm, c), sl(cm, c), sl(dt, c), a_neg) for c in range(n_chunks)]
    st = st_ref[...]
    ys = []
    for c, (y_intra, ecum_e, upd) in enumerate(pre):
        ys.append(y_intra + _dot(sl(cm, c), st) * ecum_e)
        st = st * ecum_e[L - 1:L, :] + upd
    st_ref[...] = st

    z = z_ref[...].astype(F32)
    y = (jnp.concatenate(ys, axis=0) + pw[5:6, :] * xs) * (z * _sigmoid(z))
    y = y * lax.rsqrt(jnp.mean(y * y, axis=-1, keepdims=True) + RMS_EPS)
    y_ref[...] = (y * pw[6:7, :]).astype(y_ref.dtype)


def _ssd(proj, pwide, psmall, bsz, seq, tl):
    nt = seq // tl
    t = bsz * seq
    gw = SSM_GROUP_W
    blk = lambda off, w: (lambda b, g, i: (b * nt + i, off // w + g))
    return pl.pallas_call(
        functools.partial(_ssd_kernel, tl // SSD_CHUNK),
        grid=(bsz, SSM_GROUPS, nt),
        in_specs=[pl.BlockSpec((tl, gw), blk(OFF_SSM_Z, gw)),
                  pl.BlockSpec((tl, gw), blk(OFF_XS, gw)),
                  pl.BlockSpec((tl, LANES), blk(OFF_B, LANES)),
                  pl.BlockSpec((tl, LANES), blk(OFF_C, LANES)),
                  pl.BlockSpec((tl, LANES), blk(OFF_DT, LANES)),
                  pl.BlockSpec((16, gw), lambda b, g, i: (0, g)),
                  pl.BlockSpec((16, LANES), lambda b, g, i: (0, g))],
        out_specs=pl.BlockSpec((tl, gw), lambda b, g, i: (b * nt + i, g)),
        out_shape=jax.ShapeDtypeStruct((t, SSM_WIDTH), BF16),
        scratch_shapes=[pltpu.VMEM((SSM_STATE, gw), F32),
                        pltpu.VMEM((SUBLANES + tl, gw), F32),
                        pltpu.VMEM((SUBLANES + tl, LANES), F32),
                        pltpu.VMEM((SUBLANES + tl, LANES), F32)],
        compiler_params=pltpu.CompilerParams(
            dimension_semantics=("arbitrary", "arbitrary", "arbitrary"),
            vmem_limit_bytes=VMEM_LIMIT),
        name="ssd",
    )(proj, proj, proj, proj, proj, pwide, psmall)


def _merge_kernel(final, yr_ref, ys_ref, gr_ref, gs_ref, x_ref, wr_ref, ws_ref, wo_ref, fw_ref, o_ref):
    pr = jnp.dot(yr_ref[...], wr_ref[...], preferred_element_type=F32)
    ps = jnp.dot(ys_ref[...], ws_ref[...], preferred_element_type=F32)
    h = _sigmoid(gr_ref[...].astype(F32)) * pr + _sigmoid(gs_ref[...].astype(F32)) * ps
    o = x_ref[...] + jnp.dot(h.astype(BF16), wo_ref[...], preferred_element_type=F32)
    if final:
        o = o * lax.rsqrt(jnp.mean(o * o, axis=-1, keepdims=True) + RMS_EPS) * fw_ref[...]
    o_ref[...] = o


def _merge(y_rwkv, y_ssm, proj, x2, wr, ws, wo, fw, final, tm):
    t, d = x2.shape
    const = lambda i: (0, 0)
    return pl.pallas_call(
        functools.partial(_merge_kernel, final),
        grid=(t // tm,),
        in_specs=[pl.BlockSpec((tm, RWKV_WIDTH), lambda i: (i, 0)),
                  pl.BlockSpec((tm, SSM_WIDTH), lambda i: (i, 0)),
                  pl.BlockSpec((tm, d), lambda i: (i, OFF_G_RWKV // d)),
                  pl.BlockSpec((tm, d), lambda i: (i, OFF_G_SSM // d)),
                  pl.BlockSpec((tm, d), lambda i: (i, 0)),
                  pl.BlockSpec((RWKV_WIDTH, d), const),
                  pl.BlockSpec((SSM_WIDTH, d), const),
                  pl.BlockSpec((d, d), const),
                  pl.BlockSpec((1, d), const)],
        out_specs=pl.BlockSpec((tm, d), lambda i: (i, 0)),
        out_shape=jax.ShapeDtypeStruct((t, d), F32),
        compiler_params=pltpu.CompilerParams(
            dimension_semantics=("arbitrary",), vmem_limit_bytes=VMEM_LIMIT),
        name="merge",
    )(y_rwkv, y_ssm, proj, proj, x2, wr, ws, wo, fw)


def _pad_cols(w, width):
    return jnp.pad(w, ((0, 0), (0, width - w.shape[1])))


def _pad_rows(w, rows):
    return jnp.pad(w, ((0, rows - w.shape[0]), (0, 0)))


def _proj_weight(w_in, w_vres):
    d = w_in.shape[0]
    c = lambda a, b: w_in[:, a:b]
    rw = 0
    r, k, v = c(rw, rw + 1024), c(rw + 1024, rw + 2048), c(rw + 2048, rw + 3072)
    lora = c(rw + 3072, rw + 3200)
    z_rwkv = c(rw + 3200, rw + 4224)
    sm = 4224
    ssm_z = c(sm, sm + 2048)
    xs = c(sm + 2048, sm + 4096)
    bm = c(sm + 4096, sm + 4608)
    cm = c(sm + 4608, sm + 5120)
    dt = c(sm + 5120, sm + 5152)
    gt = sm + 5152
    g_rwkv, g_ssm = c(gt, gt + 1024), c(gt + 1024, gt + 2048)
    vres = jnp.zeros((d, LANES), w_in.dtype) if w_vres is None else _pad_cols(w_vres, LANES)
    dt4 = jnp.pad(dt.reshape(d, SSM_GROUPS, SSM_GROUP_HEADS),
                  ((0, 0), (0, 0), (0, LANES - SSM_GROUP_HEADS))).reshape(d, SSM_GROUPS * LANES)
    w = jnp.concatenate([ssm_z, g_rwkv, g_ssm, r, k, v, z_rwkv, xs, bm, cm, lora, vres, dt4], axis=1)
    return _pad_cols(w, PROJ_W).astype(BF16)


def _group_lanes(vec):
    return jnp.pad(vec.reshape(SSM_GROUPS, SSM_GROUP_HEADS),
                   ((0, 0), (0, LANES - SSM_GROUP_HEADS))).reshape(1, SSM_GROUPS * LANES)


def kernel(x, norm_w, w_in, w_in_vres, mu_rwkv, mu_vres, decay_w0, decay_w2, iclr_a0, iclr_a2, vres_v0, vres_v2, k_k, k_a, r_k, gn_w, gn_b, w_out_rwkv, conv_w, conv_b, dt_bias, a_log, d_skip, ssm_norm_w, w_out_ssm, w_out, final_norm_w):
    bsz, seq, d = x.shape
    depth = norm_w.shape[0]
    t = bsz * seq
    x2 = x.reshape(t, d)
    tm_proj = min(1024, t)
    tn_proj = 3072
    tl_rwkv = min(512, seq)
    tl_ssd = min(1024, seq)
    tm_merge = min(512, t)
    proj_dtype = BF16

    v_first = None
    for i in range(depth):
        w = _proj_weight(w_in[i], None if i == 0 else w_in_vres[i - 1])
        proj = _inproj(x2, norm_w[i].reshape(1, d), w, tm_proj, tn_proj, proj_dtype)

        mu = mu_rwkv[i]
        rows = [mu[0:1024], mu[1024:2048], mu[2048:3072], mu[3200:4224], decay_w0[i], iclr_a0[i],
                k_k[i], k_a[i], r_k[i].reshape(-1), gn_w[i], gn_b[i],
                vres_v0[i - 1] if i > 0 else jnp.zeros((RWKV_WIDTH,), F32)]
        pvec = _pad_rows(jnp.stack(rows, axis=0), 16)
        mu_small = jnp.stack([mu[3072:3200],
                              jnp.pad(mu_vres[i - 1], (0, LANES - VRES_LORA)) if i > 0
                              else jnp.zeros((LANES,), F32)], axis=0)
        mu_small = _pad_rows(mu_small, SUBLANES)
        w2p = _pad_rows(decay_w2[i], LANES).astype(BF16)
        a2p = jnp.concatenate([jnp.zeros_like(iclr_a2[i]), iclr_a2[i]], axis=0).astype(BF16)
        v2p = _pad_rows(vres_v2[i - 1], LANES).astype(BF16) if i > 0 else None
        y_rwkv, v_first = _rwkv(proj, v_first, pvec, mu_small, w2p, a2p, v2p, bsz, seq, tl_rwkv)

        cw, cb = conv_w[i], conv_b[i]
        pwide = jnp.concatenate([cw[:, :SSM_WIDTH], cb[None, :SSM_WIDTH],
                                 jnp.repeat(d_skip[i], SSM_HEAD_DIM)[None, :],
                                 ssm_norm_w[i][None, :]], axis=0)
        pwide = _pad_rows(pwide, 16)
        nb = SSM_GROUPS * SSM_STATE
        psmall = jnp.concatenate([cw[:, SSM_WIDTH:SSM_WIDTH + nb], cb[None, SSM_WIDTH:SSM_WIDTH + nb],
                                  cw[:, SSM_WIDTH + nb:], cb[None, SSM_WIDTH + nb:],
                                  _group_lanes(dt_bias[i]), _group_lanes(a_log[i])], axis=0)
        psmall = _pad_rows(psmall, 16)
        y_ssm = _ssd(proj, pwide, psmall, bsz, seq, tl_ssd)

        final = i == depth - 1
        x2 = _merge(y_rwkv, y_ssm, proj, x2, w_out_rwkv[i].astype(BF16), w_out_ssm[i].astype(BF16),
                    w_out[i].astype(BF16), final_norm_w.reshape(1, d), final, tm_merge)
    return x2.reshape(bsz, seq, d)
```

```python
import functools

import jax
import jax.numpy as jnp
from jax import lax
from jax.experimental import pallas as pl
from jax.experimental.pallas import tpu as pltpu

F32 = jnp.float32
BF16 = jnp.bfloat16

D_MODEL = 1024
RWKV_HEADS = 16
RWKV_HEAD_DIM = 64
RWKV_WIDTH = RWKV_HEADS * RWKV_HEAD_DIM
DECAY_LORA = 64
ICLR_LORA = 64
VRES_LORA = 32
RWKV_GN_EPS = 64e-5
SSM_WIDTH = 2 * D_MODEL
SSM_HEAD_DIM = 64
SSM_HEADS = SSM_WIDTH // SSM_HEAD_DIM
SSM_GROUPS = 4
SSM_STATE = 128
CONV_WIDTH = 4
SSD_CHUNK = 128
RMS_EPS = 1e-5

LANES = 128
SUBLANES = 8
DECAY_SCALE = 0.6065306597126334
RWKV_CHUNK = 64
HEAD_GROUP = 4
GROUP_W = HEAD_GROUP * RWKV_HEAD_DIM
SSM_GROUP_W = SSM_WIDTH // SSM_GROUPS
SSM_GROUP_HEADS = SSM_HEADS // SSM_GROUPS

OFF_SSM_Z = 0
OFF_G_RWKV = 2048
OFF_G_SSM = 3072
OFF_R = 4096
OFF_K = 5120
OFF_V = 6144
OFF_Z = 7168
OFF_XS = 8192
OFF_B = 10240
OFF_C = 10752
OFF_LORA = 11264
OFF_VRES = 11392
OFF_DT = 11520
PROJ_W = 12288

VMEM_LIMIT = 56 * 1024 * 1024


def _sigmoid(x):
    return 0.5 * jnp.tanh(0.5 * x) + 0.5


def _silu(x):
    h = 0.5 * x
    return h * jnp.tanh(h) + h


def _softplus(x):
    return jnp.maximum(x, 0.0) + jnp.log(1.0 + jnp.exp(-jnp.abs(x)))


def _dot(a, b):
    return jnp.dot(a.astype(BF16), b.astype(BF16), preferred_element_type=F32)


def _dot_nt(a, b):
    return lax.dot_general(a.astype(BF16), b.astype(BF16), (((1,), (1,)), ((), ())),
                           preferred_element_type=F32)


def _dot_tn(a, b):
    return lax.dot_general(a.astype(BF16), b.astype(BF16), (((0,), (0,)), ((), ())),
                           preferred_element_type=F32)


def _split(x, passes):
    parts = []
    for _ in range(passes - 1):
        p = x.astype(BF16)
        parts.append(p)
        x = x - p.astype(F32)
    parts.append(x.astype(BF16))
    return parts


def _dot_right01(x, w01, passes):
    return sum(jnp.dot(p, w01, preferred_element_type=F32) for p in _split(x, passes))


def _dot_left01(w01, x, passes):
    return sum(jnp.dot(w01, p, preferred_element_type=F32) for p in _split(x, passes))


def _iota(shape, axis):
    return lax.broadcasted_iota(jnp.int32, shape, axis)


def _delayed_rows(ext_ref, u, delays, advance=None):
    tl = u.shape[0]
    ext_ref[SUBLANES:, :] = u
    out = [ext_ref[SUBLANES - s:SUBLANES - s + tl, :] for s in delays]
    last = u[tl - SUBLANES:tl, :]
    ext_ref[0:SUBLANES, :] = last if advance is None else jnp.where(advance, last, ext_ref[0:SUBLANES, :])
    return out


def _interleave(rounds, *gens_and_paces):
    done = [0.0] * len(gens_and_paces)
    for _ in range(rounds):
        for j, (g, pace) in enumerate(gens_and_paces):
            done[j] += pace
            while done[j] >= 1.0:
                done[j] -= 1.0
                next(g, None)
    for g, _ in gens_and_paces:
        for _ in g:
            pass


def _inproj_kernel(x_ref, nw_ref, w_ref, o_ref, xn_ref):
    @pl.when(pl.program_id(1) == 0)
    def _():
        x = x_ref[...]
        ms = jnp.mean(x * x, axis=-1, keepdims=True)
        xn_ref[...] = (x * lax.rsqrt(ms + RMS_EPS) * nw_ref[...]).astype(BF16)

    o_ref[...] = jnp.dot(xn_ref[...], w_ref[...], preferred_element_type=F32).astype(o_ref.dtype)


def _inproj(x2, nw, w, tm, tn, out_dtype):
    t, d = x2.shape
    npad = w.shape[1]
    return pl.pallas_call(
        _inproj_kernel,
        grid=(t // tm, npad // tn),
        in_specs=[
            pl.BlockSpec((tm, d), lambda i, j: (i, 0)),
            pl.BlockSpec((1, d), lambda i, j: (0, 0)),
            pl.BlockSpec((d, tn), lambda i, j: (0, j)),
        ],
        out_specs=pl.BlockSpec((tm, tn), lambda i, j: (i, j)),
        out_shape=jax.ShapeDtypeStruct((t, npad), out_dtype),
        scratch_shapes=[pltpu.VMEM((tm, d), BF16)],
        compiler_params=pltpu.CompilerParams(
            dimension_semantics=("arbitrary", "arbitrary"), vmem_limit_bytes=VMEM_LIMIT),
        name="inproj",
    )(x2, nw, w)


def _block_diag(x, lane_head):
    return jnp.concatenate(
        [jnp.where(lane_head == h, x, 0.0) for h in range(HEAD_GROUP)], axis=0)


def _rwkv_chunk_pre(chunks, out):
    L = RWKV_CHUNK
    W = GROUP_W
    row = _iota((L, W), 0)
    lane = _iota((L, W), 1)
    lane_head = lane >> 6
    lane_in = lane & (RWKV_HEAD_DIM - 1)
    incl = lane_in <= row
    strict = lane_in < row
    eye = jnp.where(lane_in == row, 1.0, 0.0).astype(F32)
    tri = jnp.where(_iota((L, L), 1) <= _iota((L, L), 0), 1.0, 0.0).astype(BF16)

    bd = functools.partial(_block_diag, lane_head=lane_head)
    n = range(len(chunks))
    r, k, v, ld, av, bv = ([c[i] for c in chunks] for i in range(6))

    cum = [_dot_left01(tri, ld[i], 2) for i in n]
    g_last = [jnp.exp(cum[i][L - 1:L, :]) for i in n]
    r_t = [r[i] * jnp.exp(cum[i]) for i in n]
    a_t = [av[i] * jnp.exp(cum[i] - ld[i]) for i in n]
    g_inv = [jnp.exp(-cum[i]) for i in n]
    b_t = [bv[i] * g_inv[i] for i in n]
    k_t = [k[i] * g_inv[i] for i in n]
    g_tail = [g_last[i] * g_inv[i] for i in n]
    b_e = [bv[i] * g_tail[i] for i in n]
    k_e = [k[i] * g_tail[i] for i in n]
    yield

    prod = [_dot_nt(jnp.concatenate([a_t[i], r_t[i]], axis=0),
                    jnp.concatenate([bd(b_t[i]), bd(k_t[i])], axis=0)) for i in n]
    m_ab = [jnp.where(strict, prod[i][:L, :W], 0.0) for i in n]
    m_ak = [jnp.where(strict, prod[i][:L, W:], 0.0) for i in n]
    p_rb = [jnp.where(incl, prod[i][L:, :W], 0.0) for i in n]
    p_rk = [jnp.where(incl, prod[i][L:, W:], 0.0) for i in n]
    yield

    x = [_dot(m_ab[i], bd(m_ab[i])) for i in n]
    t_inv = [eye + m_ab[i] for i in n]
    mp_v = [_dot(jnp.concatenate([m_ak[i], p_rk[i]], axis=0), bd(v[i])) for i in n]
    yield
    for _ in range(4):
        q = [_dot(jnp.concatenate([x[i], t_inv[i]], axis=0), bd(x[i])) for i in n]
        x = [q[i][:L] for i in n]
        t_inv = [t_inv[i] + q[i][L:] for i in n]
        yield
    t_inv = [t_inv[i] + _dot(t_inv[i], bd(x[i])) for i in n]
    yield

    au = [_dot(t_inv[i], jnp.concatenate([bd(a_t[i]), bd(mp_v[i][:L])], axis=1)) for i in n]
    yield
    ry = [_dot(p_rb[i], jnp.concatenate([bd(au[i][:, :W]), bd(au[i][:, W:])], axis=1)) for i in n]
    r_hat = [r_t[i] + ry[i][:, :W] for i in n]
    y0 = [ry[i][:, W:] + mp_v[i][L:] for i in n]
    out.extend(dict(ra=jnp.concatenate([r_hat[i], au[i][:, :W]], axis=0), y0=y0[i], u0=au[i][:, W:],
                    v=v[i], bk_e=jnp.concatenate([b_e[i], k_e[i]], axis=0),
                    g_last=g_last[i]) for i in n)


def _rwkv_chunk_state(s, c, same_head):
    L = RWKV_CHUNK
    ra = _dot_nt(c["ra"], s)
    yield None
    y = ra[:L] + c["y0"]
    uv = jnp.concatenate([(ra[L:] + c["u0"]).astype(BF16), c["v"].astype(BF16)], axis=0)
    s = s * c["g_last"] + jnp.where(same_head, _dot_tn(uv, c["bk_e"]), 0.0)
    yield y, s


def _rwkv_kernel(has_vres, n_chunks, nt, *refs):
    it = iter(refs)
    r_ref, k_ref, v_ref, z_ref, lora_ref = (next(it) for _ in range(5))
    vres_ref = next(it) if has_vres else None
    vf_ref = next(it) if has_vres else None
    pv_ref, pvp_ref, mul_ref, w2_ref, a2_ref = (next(it) for _ in range(5))
    v2_ref = next(it) if has_vres else None
    y_ref = next(it)
    vout_ref = None if has_vres else next(it)
    s_ref, cr, ck, cv, cz, cl, cvr = (next(it) for _ in range(7))
    st_ra, st_y0, st_u0, st_v, st_bke, st_gl, st_bon, st_zg = (next(it) for _ in range(8))
    stash = (st_ra, st_y0, st_u0, st_v, st_bke, st_gl, st_bon, st_zg)

    n = pl.program_id(0)
    last = pl.num_programs(0) - 2
    t_cur = jnp.minimum(n, last) % nt
    t_prev = jnp.maximum(n - 1, 0) % nt
    wslot = n % 2
    rslot = 1 - wslot
    advance = n < last

    @pl.when(n == 0)
    def _():
        s_ref[...] = jnp.zeros_like(s_ref)
        for st in stash:
            st[1] = jnp.zeros(st.shape[1:], st.dtype)

    @pl.when(t_cur == 0)
    def _():
        for c in (cr, ck, cv, cz, cl, cvr):
            c[0:SUBLANES, :] = jnp.zeros((SUBLANES, c.shape[1]), c.dtype)

    pv = pv_ref[...]
    prm = lambda i: pv[i:i + 1, :]
    mu_r, mu_k, mu_v, mu_z, w0, a0, k_k, k_a, r_k, _, _, v0 = (prm(i) for i in range(12))
    gn_w, gn_b = pvp_ref[9:10, :], pvp_ref[10:11, :]

    row = _iota((GROUP_W, GROUP_W), 0)
    lane = _iota((GROUP_W, GROUP_W), 1)
    same_head = (row >> 6) == (lane >> 6)
    seg_ones = jnp.where(same_head, 1.0, 0.0).astype(BF16)
    head_sum = lambda x, passes=2: _dot_right01(x, seg_ones, passes)
    L = RWKV_CHUNK

    tl = r_ref.shape[0]

    def current_block(r0, r1):
        def shifted(ref, ext, mu):
            u = ref[r0:r1, :].astype(F32)
            ext[SUBLANES + r0:SUBLANES + r1, :] = u
            prev = ext[SUBLANES - 1 + r0:SUBLANES - 1 + r1, :]
            if r1 == tl:
                ext[0:SUBLANES, :] = jnp.where(advance, u[r1 - r0 - SUBLANES:, :], ext[0:SUBLANES, :])
            return u + (prev - u) * mu

        c0, nc = r0 // L, (r1 - r0) // L
        r = shifted(r_ref, cr, mu_r)
        k = shifted(k_ref, ck, mu_k)
        yield
        v = shifted(v_ref, cv, mu_v)
        lora = shifted(lora_ref, cl, mul_ref[0:1, :])
        yield
        ld = -DECAY_SCALE * _sigmoid(w0 + _dot(jnp.tanh(lora), w2_ref[...]))
        a = _sigmoid(a0 + _dot(lora, a2_ref[...]))
        yield
        kk = k * k_k
        kk = kk * lax.rsqrt(jnp.maximum(head_sum(kk * kk, 1), 1e-24))
        k = k * (1.0 + (a - 1.0) * k_a)
        yield
        if has_vres:
            vres = shifted(vres_ref, cvr, mul_ref[1:2, :])
            v = v + (vf_ref[r0:r1, :] - v) * _sigmoid(v0 + _dot(vres, v2_ref[...]))
        else:
            vout_ref[r0:r1, :] = v
        yield
        z = shifted(z_ref, cz, mu_z)
        st_zg[wslot, r0:r1, :] = _silu(z)
        st_bon[wslot, r0:r1, :] = head_sum(r * k * r_k) * v
        yield
        av = -kk
        bv = kk * a
        ins = [[x[c * L:(c + 1) * L, :] for x in (r, k, v, ld, av, bv)] for c in range(nc)]
        out = []
        yield from _rwkv_chunk_pre(ins, out)
        for c, o in enumerate(out, start=c0):
            st_ra[wslot, 2 * c * L:2 * (c + 1) * L, :] = o["ra"].astype(BF16)
            st_bke[wslot, 2 * c * L:2 * (c + 1) * L, :] = o["bk_e"].astype(BF16)
            st_y0[wslot, c * L:(c + 1) * L, :] = o["y0"]
            st_u0[wslot, c * L:(c + 1) * L, :] = o["u0"]
            st_v[wslot, c * L:(c + 1) * L, :] = o["v"].astype(BF16)
            st_gl[wslot, c * SUBLANES:(c + 1) * SUBLANES, :] = jnp.broadcast_to(
                o["g_last"], (SUBLANES, GROUP_W))

    def previous_block():
        s = jnp.where(t_prev == 0, 0.0, s_ref[...])
        ys = []
        for c in range(n_chunks):
            cd = dict(ra=st_ra[rslot, 2 * c * L:2 * (c + 1) * L, :],
                      bk_e=st_bke[rslot, 2 * c * L:2 * (c + 1) * L, :],
                      y0=st_y0[rslot, c * L:(c + 1) * L, :], u0=st_u0[rslot, c * L:(c + 1) * L, :],
                      v=st_v[rslot, c * L:(c + 1) * L, :],
                      g_last=st_gl[rslot, c * SUBLANES:c * SUBLANES + 1, :])
            for res in _rwkv_chunk_state(s, cd, same_head):
                yield
            y, s = res
            ys.append(y)
        s_ref[...] = s
        y = jnp.concatenate(ys, axis=0)
        inv_n = 1.0 / RWKV_HEAD_DIM
        mean = head_sum(y) * inv_n
        yc = y - mean
        yield
        var = head_sum(yc * yc, 1) * inv_n
        y = yc * lax.rsqrt(var + RWKV_GN_EPS) * gn_w + gn_b
        y_ref[...] = ((y + st_bon[rslot]) * st_zg[rslot]).astype(y_ref.dtype)

    _interleave(20, (current_block(0, tl), 1.0), (previous_block(), 1.0))


def _rwkv(proj, v_first, pvec, mul, w2p, a2p, v2p, bsz, seq, tl):
    has_vres = v_first is not None
    nt = seq // tl
    t = bsz * seq
    gw = GROUP_W
    hg = RWKV_HEADS // HEAD_GROUP
    n_blocks = bsz * hg * nt
    cur = lambda n: jnp.minimum(n, n_blocks - 1)
    prv = lambda n: jnp.maximum(n - 1, 0)
    rows = lambda m: (m // (hg * nt)) * nt + m % nt
    grp = lambda m: (m // nt) % hg
    colblk = lambda off: (lambda n: (rows(cur(n)), off // gw + grp(cur(n))))
    smallblk = lambda off: (lambda n: (rows(cur(n)), off // LANES))
    in_specs = [pl.BlockSpec((tl, gw), colblk(OFF_R)),
                pl.BlockSpec((tl, gw), colblk(OFF_K)),
                pl.BlockSpec((tl, gw), colblk(OFF_V)),
                pl.BlockSpec((tl, gw), colblk(OFF_Z)),
                pl.BlockSpec((tl, LANES), smallblk(OFF_LORA))]
    args = [proj, proj, proj, proj, proj]
    if has_vres:
        in_specs += [pl.BlockSpec((tl, LANES), smallblk(OFF_VRES)),
                     pl.BlockSpec((tl, gw), colblk(0))]
        args += [proj, v_first]
    in_specs += [pl.BlockSpec((16, gw), lambda n: (0, grp(cur(n)))),
                 pl.BlockSpec((16, gw), lambda n: (0, grp(prv(n)))),
                 pl.BlockSpec((SUBLANES, LANES), lambda n: (0, 0)),
                 pl.BlockSpec((LANES, gw), lambda n: (0, grp(cur(n)))),
                 pl.BlockSpec((LANES, gw), lambda n: (0, grp(cur(n))))]
    args += [pvec, pvec, mul, w2p, a2p]
    if has_vres:
        in_specs += [pl.BlockSpec((LANES, gw), lambda n: (0, grp(cur(n))))]
        args += [v2p]
    out_shape = [jax.ShapeDtypeStruct((t, RWKV_WIDTH), BF16)]
    out_specs = [pl.BlockSpec((tl, gw), lambda n: (rows(prv(n)), grp(prv(n))))]
    if not has_vres:
        out_shape.append(jax.ShapeDtypeStruct((t, RWKV_WIDTH), F32))
        out_specs.append(pl.BlockSpec((tl, gw), colblk(0)))
    n_chunks = tl // RWKV_CHUNK
    stash = lambda rows, dtype: pltpu.VMEM((2, rows, gw), dtype)
    scratch = ([pltpu.VMEM((gw, gw), F32)]
               + [pltpu.VMEM((SUBLANES + tl, gw), F32)] * 4
               + [pltpu.VMEM((SUBLANES + tl, LANES), F32)] * 2
               + [stash(2 * tl, BF16), stash(tl, F32), stash(tl, F32), stash(tl, BF16),
                  stash(2 * tl, BF16), stash(SUBLANES * n_chunks, F32), stash(tl, F32), stash(tl, F32)])
    res = pl.pallas_call(
        functools.partial(_rwkv_kernel, has_vres, n_chunks, nt),
        grid=(n_blocks + 1,),
        in_specs=in_specs, out_specs=out_specs, out_shape=out_shape,
        scratch_shapes=scratch,
        compiler_params=pltpu.CompilerParams(
            dimension_semantics=("arbitrary",), vmem_limit_bytes=VMEM_LIMIT),
        name="rwkv",
    )(*args)
    return (res[0], v_first) if has_vres else (res[0], res[1])


def _ssd_chunk_pre(xs, bm, cm, dt, a_neg):
    L = SSD_CHUNK
    gw = SSM_GROUP_W
    tri = jnp.where(_iota((L, L), 1) <= _iota((L, L), 0), 1.0, 0.0).astype(BF16)
    causal = _iota((L, L), 1) <= _iota((L, L), 0)
    lane_lo = _iota((L, LANES), 1) < SSM_HEAD_DIM
    heads = range(SSM_GROUP_HEADS)
    pairs = range(SSM_GROUP_HEADS // 2)

    def per_head_lanes(cols):
        return jnp.concatenate([jnp.where(lane_lo, cols[2 * p], cols[2 * p + 1]) for p in pairs], axis=1)

    cum = _dot_left01(tri, dt * a_neg, 3)
    cum_b = [jnp.broadcast_to(cum[:, h:h + 1], (L, LANES)) for h in heads]
    expand = jnp.where(_iota((LANES, gw), 0) == (_iota((LANES, gw), 1) >> 6), 1.0, 0.0).astype(BF16)
    dt_e = _dot_right01(dt, expand, 2)
    cum_e = per_head_lanes(cum_b)
    ecum_e = jnp.exp(cum_e)
    tail_e = jnp.exp(cum_e[L - 1:L, :] - cum_e)
    xdt = xs * dt_e

    cb = _dot_nt(cm, bm)
    cum_t = cum.T
    pieces = []
    for pair in pairs:
        x_pair = xdt[:, pair * LANES:(pair + 1) * LANES]
        outs = []
        for hh in (2 * pair, 2 * pair + 1):
            seg = cum_b[hh] - cum_t[hh:hh + 1, :]
            w_h = cb * jnp.exp(jnp.where(causal, seg, -jnp.inf))
            outs.append(_dot(w_h, x_pair))
        pieces.append(jnp.where(lane_lo, outs[0], outs[1]))
    y_intra = jnp.concatenate(pieces, axis=1)

    return y_intra, ecum_e, _dot_tn(bm, xdt * tail_e)


def _ssd_kernel(n_chunks, z_ref, x_ref, b_ref, c_ref, dt_ref, pw_ref, ps_ref, y_ref,
                st_ref, cx, cb_c, cc_c, px, pb_c, pc_c):
    @pl.when(pl.program_id(2) == 0)
    def _():
        st_ref[...] = jnp.zeros_like(st_ref)
        for c in (cx, cb_c, cc_c, px, pb_c, pc_c):
            c[0:SUBLANES, :] = jnp.zeros((SUBLANES, c.shape[1]), c.dtype)

    pw = pw_ref[...]
    ps = ps_ref[...]

    def conv_silu(ref, ext, ext_p, taps, bias):
        assert CONV_WIDTH == 4
        u = ref[...].astype(F32)
        u1, = _delayed_rows(ext, u, (1,))
        p2, = _delayed_rows(ext_p, u * taps[1] + u1 * taps[0], (2,))
        return _silu(u * taps[3] + u1 * taps[2] + p2 + bias)

    xs = conv_silu(x_ref, cx, px, [pw[i:i + 1, :] for i in range(4)], pw[4:5, :])
    bm = conv_silu(b_ref, cb_c, pb_c, [ps[i:i + 1, :] for i in range(4)], ps[4:5, :])
    cm = conv_silu(c_ref, cc_c, pc_c, [ps[i:i + 1, :] for i in range(5, 9)], ps[9:10, :])
    dt = _softplus(dt_ref[...].astype(F32) + ps[10:11, :])
    a_neg = -jnp.exp(ps[11:12, :])

    L = SSD_CHUNK
    sl = lambda t, c: t[c * L:(c + 1) * L, :]
    pre = [_ssd_chunk_pre(sl(xs, c), sl(bm, c), sl(cm, c), sl(dt, c), a_neg) for c in range(n_chunks)]
    st = st_ref[...]
    ys = []
    for c, (y_intra, ecum_e, upd) in enumerate(pre):
        ys.append(y_intra + _dot(sl(cm, c), st) * ecum_e)
        st = st * ecum_e[L - 1:L, :] + upd
    st_ref[...] = st

    z = z_ref[...].astype(F32)
    y = (jnp.concatenate(ys, axis=0) + pw[5:6, :] * xs) * _silu(z)
    y = y * lax.rsqrt(jnp.mean(y * y, axis=-1, keepdims=True) + RMS_EPS)
    y_ref[...] = (y * pw[6:7, :]).astype(y_ref.dtype)


def _ssd(proj, pwide, psmall, bsz, seq, tl):
    nt = seq // tl
    t = bsz * seq
    gw = SSM_GROUP_W
    blk = lambda off, w: (lambda b, g, i: (b * nt + i, off // w + g))
    return pl.pallas_call(
        functools.partial(_ssd_kernel, tl // SSD_CHUNK),
        grid=(bsz, SSM_GROUPS, nt),
        in_specs=[pl.BlockSpec((tl, gw), blk(OFF_SSM_Z, gw)),
                  pl.BlockSpec((tl, gw), blk(OFF_XS, gw)),
                  pl.BlockSpec((tl, LANES), blk(OFF_B, LANES)),
                  pl.BlockSpec((tl, LANES), blk(OFF_C, LANES)),
                  pl.BlockSpec((tl, LANES), blk(OFF_DT, LANES)),
                  pl.BlockSpec((16, gw), lambda b, g, i: (0, g)),
                  pl.BlockSpec((16, LANES), lambda b, g, i: (0, g))],
        out_specs=pl.BlockSpec((tl, gw), lambda b, g, i: (b * nt + i, g)),
        out_shape=jax.ShapeDtypeStruct((t, SSM_WIDTH), BF16),
        scratch_shapes=[pltpu.VMEM((SSM_STATE, gw), F32),
                        pltpu.VMEM((SUBLANES + tl, gw), F32),
                        pltpu.VMEM((SUBLANES + tl, LANES), F32),
                        pltpu.VMEM((SUBLANES + tl, LANES), F32),
                        pltpu.VMEM((SUBLANES + tl, gw), F32),
                        pltpu.VMEM((SUBLANES + tl, LANES), F32),
                        pltpu.VMEM((SUBLANES + tl, LANES), F32)],
        compiler_params=pltpu.CompilerParams(
            dimension_semantics=("arbitrary", "arbitrary", "arbitrary"),
            vmem_limit_bytes=VMEM_LIMIT),
        name="ssd",
    )(proj, proj, proj, proj, proj, pwide, psmall)


def _merge_kernel(final, yr_ref, ys_ref, gr_ref, gs_ref, x_ref, wr_ref, ws_ref, wo_ref, fw_ref, o_ref):
    pr = jnp.dot(yr_ref[...], wr_ref[...], preferred_element_type=F32)
    ps = jnp.dot(ys_ref[...], ws_ref[...], preferred_element_type=F32)
    h = _sigmoid(gr_ref[...].astype(F32)) * pr + _sigmoid(gs_ref[...].astype(F32)) * ps
    o = x_ref[...] + jnp.dot(h.astype(BF16), wo_ref[...], preferred_element_type=F32)
    if final:
        o = o * lax.rsqrt(jnp.mean(o * o, axis=-1, keepdims=True) + RMS_EPS) * fw_ref[...]
    o_ref[...] = o


def _merge(y_rwkv, y_ssm, proj, x2, wr, ws, wo, fw, final, tm):
    t, d = x2.shape
    const = lambda i: (0, 0)
    return pl.pallas_call(
        functools.partial(_merge_kernel, final),
        grid=(t // tm,),
        in_specs=[pl.BlockSpec((tm, RWKV_WIDTH), lambda i: (i, 0)),
                  pl.BlockSpec((tm, SSM_WIDTH), lambda i: (i, 0)),
                  pl.BlockSpec((tm, d), lambda i: (i, OFF_G_RWKV // d)),
                  pl.BlockSpec((tm, d), lambda i: (i, OFF_G_SSM // d)),
                  pl.BlockSpec((tm, d), lambda i: (i, 0)),
                  pl.BlockSpec((RWKV_WIDTH, d), const),
                  pl.BlockSpec((SSM_WIDTH, d), const),
                  pl.BlockSpec((d, d), const),
                  pl.BlockSpec((1, d), const)],
        out_specs=pl.BlockSpec((tm, d), lambda i: (i, 0)),
        out_shape=jax.ShapeDtypeStruct((t, d), F32),
        compiler_params=pltpu.CompilerParams(
            dimension_semantics=("arbitrary",), vmem_limit_bytes=VMEM_LIMIT),
        name="merge",
    )(y_rwkv, y_ssm, proj, proj, x2, wr, ws, wo, fw)


def _pad_cols(w, width):
    return jnp.pad(w, ((0, 0), (0, width - w.shape[1])))


def _pad_rows(w, rows):
    return jnp.pad(w, ((0, rows - w.shape[0]), (0, 0)))


def _proj_weight(w_in, w_vres):
    d = w_in.shape[0]
    c = lambda a, b: w_in[:, a:b]
    rw = 0
    r, k, v = c(rw, rw + 1024), c(rw + 1024, rw + 2048), c(rw + 2048, rw + 3072)
    lora = c(rw + 3072, rw + 3200)
    z_rwkv = c(rw + 3200, rw + 4224)
    sm = 4224
    ssm_z = c(sm, sm + 2048)
    xs = c(sm + 2048, sm + 4096)
    bm = c(sm + 4096, sm + 4608)
    cm = c(sm + 4608, sm + 5120)
    dt = c(sm + 5120, sm + 5152)
    gt = sm + 5152
    g_rwkv, g_ssm = c(gt, gt + 1024), c(gt + 1024, gt + 2048)
    vres = jnp.zeros((d, LANES), w_in.dtype) if w_vres is None else _pad_cols(w_vres, LANES)
    dt4 = jnp.pad(dt.reshape(d, SSM_GROUPS, SSM_GROUP_HEADS),
                  ((0, 0), (0, 0), (0, LANES - SSM_GROUP_HEADS))).reshape(d, SSM_GROUPS * LANES)
    w = jnp.concatenate([ssm_z, g_rwkv, g_ssm, r, k, v, z_rwkv, xs, bm, cm, lora, vres, dt4], axis=1)
    return _pad_cols(w, PROJ_W).astype(BF16)


def _group_lanes(vec):
    return jnp.pad(vec.reshape(SSM_GROUPS, SSM_GROUP_HEADS),
                   ((0, 0), (0, LANES - SSM_GROUP_HEADS))).reshape(1, SSM_GROUPS * LANES)


def kernel(x, norm_w, w_in, w_in_vres, mu_rwkv, mu_vres, decay_w0, decay_w2, iclr_a0, iclr_a2, vres_v0, vres_v2, k_k, k_a, r_k, gn_w, gn_b, w_out_rwkv, conv_w, conv_b, dt_bias, a_log, d_skip, ssm_norm_w, w_out_ssm, w_out, final_norm_w):
    bsz, seq, d = x.shape
    depth = norm_w.shape[0]
    t = bsz * seq
    x2 = x.reshape(t, d)
    tm_proj = min(1024, t)
    tn_proj = 3072
    tl_rwkv = min(512, seq)
    tl_ssd = min(1024, seq)
    tm_merge = min(512, t)
    proj_dtype = BF16

    v_first = None
    for i in range(depth):
        w = _proj_weight(w_in[i], None if i == 0 else w_in_vres[i - 1])
        proj = _inproj(x2, norm_w[i].reshape(1, d), w, tm_proj, tn_proj, proj_dtype)

        mu = mu_rwkv[i]
        rows = [mu[0:1024], mu[1024:2048], mu[2048:3072], mu[3200:4224], decay_w0[i], iclr_a0[i],
                k_k[i], k_a[i], r_k[i].reshape(-1), gn_w[i], gn_b[i],
                vres_v0[i - 1] if i > 0 else jnp.zeros((RWKV_WIDTH,), F32)]
        pvec = _pad_rows(jnp.stack(rows, axis=0), 16)
        mu_small = jnp.stack([mu[3072:3200],
                              jnp.pad(mu_vres[i - 1], (0, LANES - VRES_LORA)) if i > 0
                              else jnp.zeros((LANES,), F32)], axis=0)
        mu_small = _pad_rows(mu_small, SUBLANES)
        w2p = _pad_rows(decay_w2[i], LANES).astype(BF16)
        a2p = jnp.concatenate([jnp.zeros_like(iclr_a2[i]), iclr_a2[i]], axis=0).astype(BF16)
        v2p = _pad_rows(vres_v2[i - 1], LANES).astype(BF16) if i > 0 else None
        y_rwkv, v_first = _rwkv(proj, v_first, pvec, mu_small, w2p, a2p, v2p, bsz, seq, tl_rwkv)

        cw, cb = conv_w[i], conv_b[i]
        pwide = jnp.concatenate([cw[:, :SSM_WIDTH], cb[None, :SSM_WIDTH],
                                 jnp.repeat(d_skip[i], SSM_HEAD_DIM)[None, :],
                                 ssm_norm_w[i][None, :]], axis=0)
        pwide = _pad_rows(pwide, 16)
        nb = SSM_GROUPS * SSM_STATE
        psmall = jnp.concatenate([cw[:, SSM_WIDTH:SSM_WIDTH + nb], cb[None, SSM_WIDTH:SSM_WIDTH + nb],
                                  cw[:, SSM_WIDTH + nb:], cb[None, SSM_WIDTH + nb:],
                                  _group_lanes(dt_bias[i]), _group_lanes(a_log[i])], axis=0)
        psmall = _pad_rows(psmall, 16)
        y_ssm = _ssd(proj, pwide, psmall, bsz, seq, tl_ssd)

        final = i == depth - 1
        x2 = _merge(y_rwkv, y_ssm, proj, x2, w_out_rwkv[i].astype(BF16), w_out_ssm[i].astype(BF16),
                    w_out[i].astype(BF16), final_norm_w.reshape(1, d), final, tm_merge)
    return x2.reshape(bsz, seq, d)
```

```python
import functools

import jax
import jax.numpy as jnp
from jax import lax
from jax.experimental import pallas as pl
from jax.experimental.pallas import tpu as pltpu

F32 = jnp.float32
BF16 = jnp.bfloat16

D_MODEL = 1024
RWKV_HEADS = 16
RWKV_HEAD_DIM = 64
RWKV_WIDTH = RWKV_HEADS * RWKV_HEAD_DIM
DECAY_LORA = 64
ICLR_LORA = 64
VRES_LORA = 32
RWKV_GN_EPS = 64e-5
SSM_WIDTH = 2 * D_MODEL
SSM_HEAD_DIM = 64
SSM_HEADS = SSM_WIDTH // SSM_HEAD_DIM
SSM_GROUPS = 4
SSM_STATE = 128
CONV_WIDTH = 4
SSD_CHUNK = 128
RMS_EPS = 1e-5

LANES = 128
SUBLANES = 8
DECAY_SCALE = 0.6065306597126334
LOG2_E = 1.4426950408889634
RWKV_CHUNK = 64
HEAD_GROUP = 4
GROUP_W = HEAD_GROUP * RWKV_HEAD_DIM
SSM_GROUP_W = SSM_WIDTH // SSM_GROUPS
SSM_GROUP_HEADS = SSM_HEADS // SSM_GROUPS

OFF_SSM_Z = 0
OFF_G_RWKV = 2048
OFF_G_SSM = 3072
OFF_R = 4096
OFF_K = 5120
OFF_V = 6144
OFF_Z = 7168
OFF_XS = 8192
OFF_B = 10240
OFF_C = 10752
OFF_LORA = 11264
OFF_VRES = 11392
OFF_DT = 11520
PROJ_W = 12288

VMEM_LIMIT = 56 * 1024 * 1024


def _sigmoid(x):
    return 0.5 * jnp.tanh(0.5 * x) + 0.5


def _silu(x):
    h = 0.5 * x
    return h * jnp.tanh(h) + h


def _softplus(x):
    return jnp.maximum(x, 0.0) + jnp.log(1.0 + jnp.exp(-jnp.abs(x)))


def _dot(a, b):
    return jnp.dot(a.astype(BF16), b.astype(BF16), preferred_element_type=F32)


def _dot_nt(a, b):
    return lax.dot_general(a.astype(BF16), b.astype(BF16), (((1,), (1,)), ((), ())),
                           preferred_element_type=F32)


def _dot_tn(a, b):
    return lax.dot_general(a.astype(BF16), b.astype(BF16), (((0,), (0,)), ((), ())),
                           preferred_element_type=F32)


def _split(x, passes):
    parts = []
    for _ in range(passes - 1):
        p = x.astype(BF16)
        parts.append(p)
        x = x - p.astype(F32)
    parts.append(x.astype(BF16))
    return parts


def _dot_right01(x, w01, passes):
    return sum(jnp.dot(p, w01, preferred_element_type=F32) for p in _split(x, passes))


def _dot_left01(w01, x, passes):
    return sum(jnp.dot(w01, p, preferred_element_type=F32) for p in _split(x, passes))


def _iota(shape, axis):
    return lax.broadcasted_iota(jnp.int32, shape, axis)


def _delayed_rows(ext_ref, u, delays, advance=None):
    tl = u.shape[0]
    ext_ref[SUBLANES:, :] = u
    out = [ext_ref[SUBLANES - s:SUBLANES - s + tl, :] for s in delays]
    last = u[tl - SUBLANES:tl, :]
    ext_ref[0:SUBLANES, :] = last if advance is None else jnp.where(advance, last, ext_ref[0:SUBLANES, :])
    return out


def _interleave(rounds, *gens_and_paces):
    done = [0.0] * len(gens_and_paces)
    for _ in range(rounds):
        for j, (g, pace) in enumerate(gens_and_paces):
            done[j] += pace
            while done[j] >= 1.0:
                done[j] -= 1.0
                next(g, None)
    for g, _ in gens_and_paces:
        for _ in g:
            pass


def _inproj_kernel(x_ref, nw_ref, w_ref, o_ref, xn_ref):
    @pl.when(pl.program_id(1) == 0)
    def _():
        x = x_ref[...]
        ms = jnp.mean(x * x, axis=-1, keepdims=True)
        xn_ref[...] = (x * lax.rsqrt(ms + RMS_EPS) * nw_ref[...]).astype(BF16)

    o_ref[...] = jnp.dot(xn_ref[...], w_ref[...], preferred_element_type=F32).astype(o_ref.dtype)


def _inproj(x2, nw, w, tm, tn, out_dtype):
    t, d = x2.shape
    npad = w.shape[1]
    return pl.pallas_call(
        _inproj_kernel,
        grid=(t // tm, npad // tn),
        in_specs=[
            pl.BlockSpec((tm, d), lambda i, j: (i, 0)),
            pl.BlockSpec((1, d), lambda i, j: (0, 0)),
            pl.BlockSpec((d, tn), lambda i, j: (0, j)),
        ],
        out_specs=pl.BlockSpec((tm, tn), lambda i, j: (i, j)),
        out_shape=jax.ShapeDtypeStruct((t, npad), out_dtype),
        scratch_shapes=[pltpu.VMEM((tm, d), BF16)],
        compiler_params=pltpu.CompilerParams(
            dimension_semantics=("arbitrary", "arbitrary"), vmem_limit_bytes=VMEM_LIMIT),
        name="inproj",
    )(x2, nw, w)


def _block_diag(x, lane_head):
    return jnp.concatenate(
        [jnp.where(lane_head == h, x, 0.0) for h in range(HEAD_GROUP)], axis=0)


def _rwkv_chunk_pre(chunks, out):
    L = RWKV_CHUNK
    W = GROUP_W
    row = _iota((L, W), 0)
    lane = _iota((L, W), 1)
    lane_head = lane >> 6
    lane_in = lane & (RWKV_HEAD_DIM - 1)
    incl = lane_in <= row
    strict = lane_in < row
    eye = jnp.where(lane_in == row, 1.0, 0.0).astype(F32)
    tri = jnp.where(_iota((L, L), 1) <= _iota((L, L), 0), 1.0, 0.0).astype(BF16)

    bd = functools.partial(_block_diag, lane_head=lane_head)
    n = range(len(chunks))
    r, k, v, ld, av, bv = ([c[i] for c in chunks] for i in range(6))

    cum = [_dot_left01(tri, ld[i], 2) for i in n]
    g_last = [jnp.exp2(cum[i][L - 1:L, :]) for i in n]
    r_t = [r[i] * jnp.exp2(cum[i]) for i in n]
    a_t = [av[i] * jnp.exp2(cum[i] - ld[i]) for i in n]
    g_inv = [jnp.exp2(-cum[i]) for i in n]
    b_t = [bv[i] * g_inv[i] for i in n]
    k_t = [k[i] * g_inv[i] for i in n]
    g_tail = [g_last[i] * g_inv[i] for i in n]
    b_e = [bv[i] * g_tail[i] for i in n]
    k_e = [k[i] * g_tail[i] for i in n]
    yield

    prod = [_dot_nt(jnp.concatenate([a_t[i], r_t[i]], axis=0),
                    jnp.concatenate([bd(b_t[i]), bd(k_t[i])], axis=0)) for i in n]
    m_ab = [jnp.where(strict, prod[i][:L, :W], 0.0) for i in n]
    m_ak = [jnp.where(strict, prod[i][:L, W:], 0.0) for i in n]
    p_rb = [jnp.where(incl, prod[i][L:, :W], 0.0) for i in n]
    p_rk = [jnp.where(incl, prod[i][L:, W:], 0.0) for i in n]
    yield

    x = [_dot(m_ab[i], bd(m_ab[i])) for i in n]
    t_inv = [eye + m_ab[i] for i in n]
    mp_v = [_dot(jnp.concatenate([m_ak[i], p_rk[i]], axis=0), bd(v[i])) for i in n]
    yield
    for _ in range(4):
        q = [_dot(jnp.concatenate([x[i], t_inv[i]], axis=0), bd(x[i])) for i in n]
        x = [q[i][:L] for i in n]
        t_inv = [t_inv[i] + q[i][L:] for i in n]
        yield
    t_inv = [t_inv[i] + _dot(t_inv[i], bd(x[i])) for i in n]
    yield

    au = [_dot(t_inv[i], jnp.concatenate([bd(a_t[i]), bd(mp_v[i][:L])], axis=1)) for i in n]
    yield
    ry = [_dot(p_rb[i], jnp.concatenate([bd(au[i][:, :W]), bd(au[i][:, W:])], axis=1)) for i in n]
    r_hat = [r_t[i] + ry[i][:, :W] for i in n]
    y0 = [ry[i][:, W:] + mp_v[i][L:] for i in n]
    out.extend(dict(ra=jnp.concatenate([r_hat[i], au[i][:, :W]], axis=0), y0=y0[i], u0=au[i][:, W:],
                    v=v[i], bk_e=jnp.concatenate([b_e[i], k_e[i]], axis=0),
                    g_last=g_last[i]) for i in n)


def _rwkv_chunk_state(s, c, same_head):
    L = RWKV_CHUNK
    ra = _dot_nt(c["ra"], s)
    yield None
    y = ra[:L] + c["y0"]
    uv = jnp.concatenate([(ra[L:] + c["u0"]).astype(BF16), c["v"].astype(BF16)], axis=0)
    s = s * c["g_last"] + jnp.where(same_head, _dot_tn(uv, c["bk_e"]), 0.0)
    yield y, s


def _rwkv_kernel(has_vres, n_chunks, nt, *refs):
    it = iter(refs)
    r_ref, k_ref, v_ref, z_ref, lora_ref = (next(it) for _ in range(5))
    vres_ref = next(it) if has_vres else None
    vf_ref = next(it) if has_vres else None
    pv_ref, pvp_ref, mul_ref, w2_ref, a2_ref = (next(it) for _ in range(5))
    v2_ref = next(it) if has_vres else None
    y_ref = next(it)
    vout_ref = None if has_vres else next(it)
    s_ref, cr, ck, cv, cz, cl, cvr = (next(it) for _ in range(7))
    st_ra, st_y0, st_u0, st_v, st_bke, st_gl, st_bon, st_zg = (next(it) for _ in range(8))
    stash = (st_ra, st_y0, st_u0, st_v, st_bke, st_gl, st_bon, st_zg)

    n = pl.program_id(0)
    last = pl.num_programs(0) - 2
    t_cur = jnp.minimum(n, last) % nt
    t_prev = jnp.maximum(n - 1, 0) % nt
    wslot = n % 2
    rslot = 1 - wslot
    advance = n < last

    @pl.when(n == 0)
    def _():
        s_ref[...] = jnp.zeros_like(s_ref)
        for st in stash:
            st[1] = jnp.zeros(st.shape[1:], st.dtype)

    @pl.when(t_cur == 0)
    def _():
        for c in (cr, ck, cv, cz, cl, cvr):
            c[0:SUBLANES, :] = jnp.zeros((SUBLANES, c.shape[1]), c.dtype)

    pv = pv_ref[...]
    prm = lambda i: pv[i:i + 1, :]
    mu_r, mu_k, mu_v, mu_z, w0, a0, k_k, k_a, r_k, _, _, v0 = (prm(i) for i in range(12))
    gn_w, gn_b = pvp_ref[9:10, :], pvp_ref[10:11, :]

    row = _iota((GROUP_W, GROUP_W), 0)
    lane = _iota((GROUP_W, GROUP_W), 1)
    same_head = (row >> 6) == (lane >> 6)
    seg_ones = jnp.where(same_head, 1.0, 0.0).astype(BF16)
    head_sum = lambda x: _dot_right01(x, seg_ones, 1)
    L = RWKV_CHUNK

    tl = r_ref.shape[0]

    def current_block(r0, r1):
        def shifted(ref, ext, mu):
            u = ref[r0:r1, :].astype(F32)
            ext[SUBLANES + r0:SUBLANES + r1, :] = u
            prev = ext[SUBLANES - 1 + r0:SUBLANES - 1 + r1, :]
            if r1 == tl:
                ext[0:SUBLANES, :] = jnp.where(advance, u[r1 - r0 - SUBLANES:, :], ext[0:SUBLANES, :])
            return u + (prev - u) * mu

        c0, nc = r0 // L, (r1 - r0) // L
        r = shifted(r_ref, cr, mu_r)
        k = shifted(k_ref, ck, mu_k)
        yield
        v = shifted(v_ref, cv, mu_v)
        lora = shifted(lora_ref, cl, mul_ref[0:1, :])
        yield
        ld = -(DECAY_SCALE * LOG2_E) * _sigmoid(w0 + _dot(jnp.tanh(lora), w2_ref[...]))
        a = _sigmoid(a0 + _dot(lora, a2_ref[...]))
        yield
        kk = k * k_k
        kk = kk * lax.rsqrt(jnp.maximum(head_sum(kk * kk), 1e-24))
        k = k * (1.0 + (a - 1.0) * k_a)
        yield
        if has_vres:
            vres = shifted(vres_ref, cvr, mul_ref[1:2, :])
            v = v + (vf_ref[r0:r1, :] - v) * _sigmoid(v0 + _dot(vres, v2_ref[...]))
        else:
            vout_ref[r0:r1, :] = v
        yield
        z = shifted(z_ref, cz, mu_z)
        st_zg[wslot, r0:r1, :] = _silu(z)
        st_bon[wslot, r0:r1, :] = head_sum(r * k * r_k) * v
        yield
        av = -kk
        bv = kk * a
        ins = [[x[c * L:(c + 1) * L, :] for x in (r, k, v, ld, av, bv)] for c in range(nc)]
        out = []
        yield from _rwkv_chunk_pre(ins, out)
        for c, o in enumerate(out, start=c0):
            st_ra[wslot, 2 * c * L:2 * (c + 1) * L, :] = o["ra"].astype(BF16)
            st_bke[wslot, 2 * c * L:2 * (c + 1) * L, :] = o["bk_e"].astype(BF16)
            st_y0[wslot, c * L:(c + 1) * L, :] = o["y0"]
            st_u0[wslot, c * L:(c + 1) * L, :] = o["u0"]
            st_v[wslot, c * L:(c + 1) * L, :] = o["v"].astype(BF16)
            st_gl[wslot, c * SUBLANES:(c + 1) * SUBLANES, :] = jnp.broadcast_to(
                o["g_last"], (SUBLANES, GROUP_W))

    def previous_block():
        s = jnp.where(t_prev == 0, 0.0, s_ref[...])
        ys = []
        for c in range(n_chunks):
            cd = dict(ra=st_ra[rslot, 2 * c * L:2 * (c + 1) * L, :],
                      bk_e=st_bke[rslot, 2 * c * L:2 * (c + 1) * L, :],
                      y0=st_y0[rslot, c * L:(c + 1) * L, :], u0=st_u0[rslot, c * L:(c + 1) * L, :],
                      v=st_v[rslot, c * L:(c + 1) * L, :],
                      g_last=st_gl[rslot, c * SUBLANES:c * SUBLANES + 1, :])
            for res in _rwkv_chunk_state(s, cd, same_head):
                yield
            y, s = res
            ys.append(y)
        s_ref[...] = s
        y = jnp.concatenate(ys, axis=0)
        inv_n = 1.0 / RWKV_HEAD_DIM
        mean = head_sum(y) * inv_n
        yc = y - mean
        yield
        var = head_sum(yc * yc) * inv_n
        y = yc * lax.rsqrt(var + RWKV_GN_EPS) * gn_w + gn_b
        y_ref[...] = ((y + st_bon[rslot]) * st_zg[rslot]).astype(y_ref.dtype)

    _interleave(20, (current_block(0, tl), 1.0), (previous_block(), 1.0))


def _rwkv(proj, v_first, pvec, mul, w2p, a2p, v2p, bsz, seq, tl):
    has_vres = v_first is not None
    nt = seq // tl
    t = bsz * seq
    gw = GROUP_W
    hg = RWKV_HEADS // HEAD_GROUP
    n_blocks = bsz * hg * nt
    cur = lambda n: jnp.minimum(n, n_blocks - 1)
    prv = lambda n: jnp.maximum(n - 1, 0)
    rows = lambda m: (m // (hg * nt)) * nt + m % nt
    grp = lambda m: (m // nt) % hg
    colblk = lambda off: (lambda n: (rows(cur(n)), off // gw + grp(cur(n))))
    smallblk = lambda off: (lambda n: (rows(cur(n)), off // LANES))
    in_specs = [pl.BlockSpec((tl, gw), colblk(OFF_R)),
                pl.BlockSpec((tl, gw), colblk(OFF_K)),
                pl.BlockSpec((tl, gw), colblk(OFF_V)),
                pl.BlockSpec((tl, gw), colblk(OFF_Z)),
                pl.BlockSpec((tl, LANES), smallblk(OFF_LORA))]
    args = [proj, proj, proj, proj, proj]
    if has_vres:
        in_specs += [pl.BlockSpec((tl, LANES), smallblk(OFF_VRES)),
                     pl.BlockSpec((tl, gw), colblk(0))]
        args += [proj, v_first]
    in_specs += [pl.BlockSpec((16, gw), lambda n: (0, grp(cur(n)))),
                 pl.BlockSpec((16, gw), lambda n: (0, grp(prv(n)))),
                 pl.BlockSpec((SUBLANES, LANES), lambda n: (0, 0)),
                 pl.BlockSpec((LANES, gw), lambda n: (0, grp(cur(n)))),
                 pl.BlockSpec((LANES, gw), lambda n: (0, grp(cur(n))))]
    args += [pvec, pvec, mul, w2p, a2p]
    if has_vres:
        in_specs += [pl.BlockSpec((LANES, gw), lambda n: (0, grp(cur(n))))]
        args += [v2p]
    out_shape = [jax.ShapeDtypeStruct((t, RWKV_WIDTH), BF16)]
    out_specs = [pl.BlockSpec((tl, gw), lambda n: (rows(prv(n)), grp(prv(n))))]
    if not has_vres:
        out_shape.append(jax.ShapeDtypeStruct((t, RWKV_WIDTH), F32))
        out_specs.append(pl.BlockSpec((tl, gw), colblk(0)))
    n_chunks = tl // RWKV_CHUNK
    stash = lambda rows, dtype: pltpu.VMEM((2, rows, gw), dtype)
    scratch = ([pltpu.VMEM((gw, gw), F32)]
               + [pltpu.VMEM((SUBLANES + tl, gw), F32)] * 4
               + [pltpu.VMEM((SUBLANES + tl, LANES), F32)] * 2
               + [stash(2 * tl, BF16), stash(tl, F32), stash(tl, F32), stash(tl, BF16),
                  stash(2 * tl, BF16), stash(SUBLANES * n_chunks, F32), stash(tl, F32), stash(tl, F32)])
    res = pl.pallas_call(
        functools.partial(_rwkv_kernel, has_vres, n_chunks, nt),
        grid=(n_blocks + 1,),
        in_specs=in_specs, out_specs=out_specs, out_shape=out_shape,
        scratch_shapes=scratch,
        compiler_params=pltpu.CompilerParams(
            dimension_semantics=("arbitrary",), vmem_limit_bytes=VMEM_LIMIT),
        name="rwkv",
    )(*args)
    return (res[0], v_first) if has_vres else (res[0], res[1])


def _ssd_chunk_pre(xs, bm, cm, dt, a_neg):
    L = SSD_CHUNK
    gw = SSM_GROUP_W
    tri = jnp.where(_iota((L, L), 1) <= _iota((L, L), 0), 1.0, 0.0).astype(BF16)
    causal = _iota((L, L), 1) <= _iota((L, L), 0)
    lane_lo = _iota((L, LANES), 1) < SSM_HEAD_DIM
    heads = range(SSM_GROUP_HEADS)
    pairs = range(SSM_GROUP_HEADS // 2)

    def per_head_lanes(cols):
        return jnp.concatenate([jnp.where(lane_lo, cols[2 * p], cols[2 * p + 1]) for p in pairs], axis=1)

    cum = _dot_left01(tri, dt * a_neg, 3)
    cum_b = [jnp.broadcast_to(cum[:, h:h + 1], (L, LANES)) for h in heads]
    expand = jnp.where(_iota((LANES, gw), 0) == (_iota((LANES, gw), 1) >> 6), 1.0, 0.0).astype(BF16)
    dt_e = _dot_right01(dt, expand, 2)
    cum_e = per_head_lanes(cum_b)
    ecum_e = jnp.exp2(cum_e)
    tail_e = jnp.exp2(cum_e[L - 1:L, :] - cum_e)
    xdt = xs * dt_e

    cb = _dot_nt(cm, bm)
    cum_t = cum.T
    pieces = []
    for pair in pairs:
        x_pair = xdt[:, pair * LANES:(pair + 1) * LANES]
        outs = []
        for hh in (2 * pair, 2 * pair + 1):
            seg = cum_b[hh] - cum_t[hh:hh + 1, :]
            w_h = cb * jnp.exp2(jnp.where(causal, seg, -jnp.inf))
            outs.append(_dot(w_h, x_pair))
        pieces.append(jnp.where(lane_lo, outs[0], outs[1]))
    y_intra = jnp.concatenate(pieces, axis=1)

    return y_intra, ecum_e, _dot_tn(bm, xdt * tail_e)


def _ssd_kernel(n_chunks, z_ref, x_ref, b_ref, c_ref, dt_ref, pw_ref, ps_ref, y_ref,
                st_ref, cx, cb_c, cc_c, px, pb_c, pc_c):
    @pl.when(pl.program_id(2) == 0)
    def _():
        st_ref[...] = jnp.zeros_like(st_ref)
        for c in (cx, cb_c, cc_c, px, pb_c, pc_c):
            c[0:SUBLANES, :] = jnp.zeros((SUBLANES, c.shape[1]), c.dtype)

    pw = pw_ref[...]
    ps = ps_ref[...]

    def conv_silu(ref, ext, ext_p, taps, bias):
        assert CONV_WIDTH == 4
        u = ref[...].astype(F32)
        u1, = _delayed_rows(ext, u, (1,))
        p2, = _delayed_rows(ext_p, u * taps[1] + u1 * taps[0], (2,))
        return _silu(u * taps[3] + u1 * taps[2] + p2 + bias)

    xs = conv_silu(x_ref, cx, px, [pw[i:i + 1, :] for i in range(4)], pw[4:5, :])
    bm = conv_silu(b_ref, cb_c, pb_c, [ps[i:i + 1, :] for i in range(4)], ps[4:5, :])
    cm = conv_silu(c_ref, cc_c, pc_c, [ps[i:i + 1, :] for i in range(5, 9)], ps[9:10, :])
    dt = _softplus(dt_ref[...].astype(F32) + ps[10:11, :])
    a_neg = -LOG2_E * jnp.exp(ps[11:12, :])

    L = SSD_CHUNK
    sl = lambda t, c: t[c * L:(c + 1) * L, :]
    pre = [_ssd_chunk_pre(sl(xs, c), sl(bm, c), sl(cm, c), sl(dt, c), a_neg) for c in range(n_chunks)]
    st = st_ref[...]
    ys = []
    for c, (y_intra, ecum_e, upd) in enumerate(pre):
        ys.append(y_intra + _dot(sl(cm, c), st) * ecum_e)
        st = st * ecum_e[L - 1:L, :] + upd
    st_ref[...] = st

    z = z_ref[...].astype(F32)
    y = (jnp.concatenate(ys, axis=0) + pw[5:6, :] * xs) * _silu(z)
    y = y * lax.rsqrt(jnp.mean(y * y, axis=-1, keepdims=True) + RMS_EPS)
    y_ref[...] = (y * pw[6:7, :]).astype(y_ref.dtype)


def _ssd(proj, pwide, psmall, bsz, seq, tl):
    nt = seq // tl
    t = bsz * seq
    gw = SSM_GROUP_W
    blk = lambda off, w: (lambda b, g, i: (b * nt + i, off // w + g))
    return pl.pallas_call(
        functools.partial(_ssd_kernel, tl // SSD_CHUNK),
        grid=(bsz, SSM_GROUPS, nt),
        in_specs=[pl.BlockSpec((tl, gw), blk(OFF_SSM_Z, gw)),
                  pl.BlockSpec((tl, gw), blk(OFF_XS, gw)),
                  pl.BlockSpec((tl, LANES), blk(OFF_B, LANES)),
                  pl.BlockSpec((tl, LANES), blk(OFF_C, LANES)),
                  pl.BlockSpec((tl, LANES), blk(OFF_DT, LANES)),
                  pl.BlockSpec((16, gw), lambda b, g, i: (0, g)),
                  pl.BlockSpec((16, LANES), lambda b, g, i: (0, g))],
        out_specs=pl.BlockSpec((tl, gw), lambda b, g, i: (b * nt + i, g)),
        out_shape=jax.ShapeDtypeStruct((t, SSM_WIDTH), BF16),
        scratch_shapes=[pltpu.VMEM((SSM_STATE, gw), F32),
                        pltpu.VMEM((SUBLANES + tl, gw), F32),
                        pltpu.VMEM((SUBLANES + tl, LANES), F32),
                        pltpu.VMEM((SUBLANES + tl, LANES), F32),
                        pltpu.VMEM((SUBLANES + tl, gw), F32),
                        pltpu.VMEM((SUBLANES + tl, LANES), F32),
                        pltpu.VMEM((SUBLANES + tl, LANES), F32)],
        compiler_params=pltpu.CompilerParams(
            dimension_semantics=("arbitrary", "arbitrary", "arbitrary"),
            vmem_limit_bytes=VMEM_LIMIT),
        name="ssd",
    )(proj, proj, proj, proj, proj, pwide, psmall)


def _merge_kernel(final, yr_ref, ys_ref, gr_ref, gs_ref, x_ref, wr_ref, ws_ref, wo_ref, fw_ref, o_ref):
    pr = jnp.dot(yr_ref[...], wr_ref[...], preferred_element_type=F32)
    ps = jnp.dot(ys_ref[...], ws_ref[...], preferred_element_type=F32)
    h = _sigmoid(gr_ref[...].astype(F32)) * pr + _sigmoid(gs_ref[...].astype(F32)) * ps
    o = x_ref[...] + jnp.dot(h.astype(BF16), wo_ref[...], preferred_element_type=F32)
    if final:
        o = o * lax.rsqrt(jnp.mean(o * o, axis=-1, keepdims=True) + RMS_EPS) * fw_ref[...]
    o_ref[...] = o


def _merge(y_rwkv, y_ssm, proj, x2, wr, ws, wo, fw, final, tm):
    t, d = x2.shape
    const = lambda i: (0, 0)
    return pl.pallas_call(
        functools.partial(_merge_kernel, final),
        grid=(t // tm,),
        in_specs=[pl.BlockSpec((tm, RWKV_WIDTH), lambda i: (i, 0)),
                  pl.BlockSpec((tm, SSM_WIDTH), lambda i: (i, 0)),
                  pl.BlockSpec((tm, d), lambda i: (i, OFF_G_RWKV // d)),
                  pl.BlockSpec((tm, d), lambda i: (i, OFF_G_SSM // d)),
                  pl.BlockSpec((tm, d), lambda i: (i, 0)),
                  pl.BlockSpec((RWKV_WIDTH, d), const),
                  pl.BlockSpec((SSM_WIDTH, d), const),
                  pl.BlockSpec((d, d), const),
                  pl.BlockSpec((1, d), const)],
        out_specs=pl.BlockSpec((tm, d), lambda i: (i, 0)),
        out_shape=jax.ShapeDtypeStruct((t, d), F32),
        compiler_params=pltpu.CompilerParams(
            dimension_semantics=("arbitrary",), vmem_limit_bytes=VMEM_LIMIT),
        name="merge",
    )(y_rwkv, y_ssm, proj, proj, x2, wr, ws, wo, fw)


def _pad_cols(w, width):
    return jnp.pad(w, ((0, 0), (0, width - w.shape[1])))


def _pad_rows(w, rows):
    return jnp.pad(w, ((0, rows - w.shape[0]), (0, 0)))


def _proj_weight(w_in, w_vres):
    d = w_in.shape[0]
    c = lambda a, b: w_in[:, a:b]
    rw = 0
    r, k, v = c(rw, rw + 1024), c(rw + 1024, rw + 2048), c(rw + 2048, rw + 3072)
    lora = c(rw + 3072, rw + 3200)
    z_rwkv = c(rw + 3200, rw + 4224)
    sm = 4224
    ssm_z = c(sm, sm + 2048)
    xs = c(sm + 2048, sm + 4096)
    bm = c(sm + 4096, sm + 4608)
    cm = c(sm + 4608, sm + 5120)
    dt = c(sm + 5120, sm + 5152)
    gt = sm + 5152
    g_rwkv, g_ssm = c(gt, gt + 1024), c(gt + 1024, gt + 2048)
    vres = jnp.zeros((d, LANES), w_in.dtype) if w_vres is None else _pad_cols(w_vres, LANES)
    dt4 = jnp.pad(dt.reshape(d, SSM_GROUPS, SSM_GROUP_HEADS),
                  ((0, 0), (0, 0), (0, LANES - SSM_GROUP_HEADS))).reshape(d, SSM_GROUPS * LANES)
    w = jnp.concatenate([ssm_z, g_rwkv, g_ssm, r, k, v, z_rwkv, xs, bm, cm, lora, vres, dt4], axis=1)
    return _pad_cols(w, PROJ_W).astype(BF16)


def _group_lanes(vec):
    return jnp.pad(vec.reshape(SSM_GROUPS, SSM_GROUP_HEADS),
                   ((0, 0), (0, LANES - SSM_GROUP_HEADS))).reshape(1, SSM_GROUPS * LANES)


def kernel(x, norm_w, w_in, w_in_vres, mu_rwkv, mu_vres, decay_w0, decay_w2, iclr_a0, iclr_a2, vres_v0, vres_v2, k_k, k_a, r_k, gn_w, gn_b, w_out_rwkv, conv_w, conv_b, dt_bias, a_log, d_skip, ssm_norm_w, w_out_ssm, w_out, final_norm_w):
    bsz, seq, d = x.shape
    depth = norm_w.shape[0]
    t = bsz * seq
    x2 = x.reshape(t, d)
    tm_proj = min(1024, t)
    tn_proj = 3072
    tl_rwkv = min(512, seq)
    tl_ssd = min(1024, seq)
    tm_merge = min(512, t)
    proj_dtype = BF16

    v_first = None
    for i in range(depth):
        w = _proj_weight(w_in[i], None if i == 0 else w_in_vres[i - 1])
        proj = _inproj(x2, norm_w[i].reshape(1, d), w, tm_proj, tn_proj, proj_dtype)

        mu = mu_rwkv[i]
        rows = [mu[0:1024], mu[1024:2048], mu[2048:3072], mu[3200:4224], decay_w0[i], iclr_a0[i],
                k_k[i], k_a[i], r_k[i].reshape(-1), gn_w[i], gn_b[i],
                vres_v0[i - 1] if i > 0 else jnp.zeros((RWKV_WIDTH,), F32)]
        pvec = _pad_rows(jnp.stack(rows, axis=0), 16)
        mu_small = jnp.stack([mu[3072:3200],
                              jnp.pad(mu_vres[i - 1], (0, LANES - VRES_LORA)) if i > 0
                              else jnp.zeros((LANES,), F32)], axis=0)
        mu_small = _pad_rows(mu_small, SUBLANES)
        w2p = _pad_rows(decay_w2[i], LANES).astype(BF16)
        a2p = jnp.concatenate([jnp.zeros_like(iclr_a2[i]), iclr_a2[i]], axis=0).astype(BF16)
        v2p = _pad_rows(vres_v2[i - 1], LANES).astype(BF16) if i > 0 else None
        y_rwkv, v_first = _rwkv(proj, v_first, pvec, mu_small, w2p, a2p, v2p, bsz, seq, tl_rwkv)

        cw, cb = conv_w[i], conv_b[i]
        pwide = jnp.concatenate([cw[:, :SSM_WIDTH], cb[None, :SSM_WIDTH],
                                 jnp.repeat(d_skip[i], SSM_HEAD_DIM)[None, :],
                                 ssm_norm_w[i][None, :]], axis=0)
        pwide = _pad_rows(pwide, 16)
        nb = SSM_GROUPS * SSM_STATE
        psmall = jnp.concatenate([cw[:, SSM_WIDTH:SSM_WIDTH + nb], cb[None, SSM_WIDTH:SSM_WIDTH + nb],
                                  cw[:, SSM_WIDTH + nb:], cb[None, SSM_WIDTH + nb:],
                                  _group_lanes(dt_bias[i]), _group_lanes(a_log[i])], axis=0)
        psmall = _pad_rows(psmall, 16)
        y_ssm = _ssd(proj, pwide, psmall, bsz, seq, tl_ssd)

        final = i == depth - 1
        x2 = _merge(y_rwkv, y_ssm, proj, x2, w_out_rwkv[i].astype(BF16), w_out_ssm[i].astype(BF16),
                    w_out[i].astype(BF16), final_norm_w.reshape(1, d), final, tm_merge)
    return x2.reshape(bsz, seq, d)
```

```python
import functools

import jax
import jax.numpy as jnp
from jax import lax
from jax.experimental import pallas as pl
from jax.experimental.pallas import tpu as pltpu

F32 = jnp.float32
BF16 = jnp.bfloat16

D_MODEL = 1024
RWKV_HEADS = 16
RWKV_HEAD_DIM = 64
RWKV_WIDTH = RWKV_HEADS * RWKV_HEAD_DIM
DECAY_LORA = 64
ICLR_LORA = 64
VRES_LORA = 32
RWKV_GN_EPS = 64e-5
SSM_WIDTH = 2 * D_MODEL
SSM_HEAD_DIM = 64
SSM_HEADS = SSM_WIDTH // SSM_HEAD_DIM
SSM_GROUPS = 4
SSM_STATE = 128
CONV_WIDTH = 4
SSD_CHUNK = 128
RMS_EPS = 1e-5

LANES = 128
SUBLANES = 8
DECAY_SCALE = 0.6065306597126334
LOG2_E = 1.4426950408889634
RWKV_CHUNK = 64
HEAD_GROUP = 4
GROUP_W = HEAD_GROUP * RWKV_HEAD_DIM
SSM_GROUP_W = SSM_WIDTH // SSM_GROUPS
SSM_GROUP_HEADS = SSM_HEADS // SSM_GROUPS

OFF_SSM_Z = 0
OFF_G_RWKV = 2048
OFF_G_SSM = 3072
OFF_R = 4096
OFF_K = 5120
OFF_V = 6144
OFF_Z = 7168
OFF_XS = 8192
OFF_B = 10240
OFF_C = 10752
OFF_LORA = 11264
OFF_VRES = 11392
OFF_DT = 11520
PROJ_W = 12288

VMEM_LIMIT = 56 * 1024 * 1024


def _sigmoid(x):
    return 0.5 * jnp.tanh(0.5 * x) + 0.5


def _silu(x):
    h = 0.5 * x
    return h * jnp.tanh(h) + h


def _softplus(x):
    return jnp.maximum(x, 0.0) + jnp.log(1.0 + jnp.exp(-jnp.abs(x)))


def _dot(a, b):
    return jnp.dot(a.astype(BF16), b.astype(BF16), preferred_element_type=F32)


def _dot_nt(a, b):
    return lax.dot_general(a.astype(BF16), b.astype(BF16), (((1,), (1,)), ((), ())),
                           preferred_element_type=F32)


def _dot_tn(a, b):
    return lax.dot_general(a.astype(BF16), b.astype(BF16), (((0,), (0,)), ((), ())),
                           preferred_element_type=F32)


def _split(x, passes):
    parts = []
    for _ in range(passes - 1):
        p = x.astype(BF16)
        parts.append(p)
        x = x - p.astype(F32)
    parts.append(x.astype(BF16))
    return parts


def _dot_right01(x, w01, passes):
    return sum(jnp.dot(p, w01, preferred_element_type=F32) for p in _split(x, passes))


def _dot_left01(w01, x, passes):
    return sum(jnp.dot(w01, p, preferred_element_type=F32) for p in _split(x, passes))


def _iota(shape, axis):
    return lax.broadcasted_iota(jnp.int32, shape, axis)


def _delayed_rows(ext_ref, u, delays, advance=None):
    tl = u.shape[0]
    ext_ref[SUBLANES:, :] = u
    out = [ext_ref[SUBLANES - s:SUBLANES - s + tl, :] for s in delays]
    last = u[tl - SUBLANES:tl, :]
    ext_ref[0:SUBLANES, :] = last if advance is None else jnp.where(advance, last, ext_ref[0:SUBLANES, :])
    return out


def _interleave(rounds, *gens_and_paces):
    done = [0.0] * len(gens_and_paces)
    for _ in range(rounds):
        for j, (g, pace) in enumerate(gens_and_paces):
            done[j] += pace
            while done[j] >= 1.0:
                done[j] -= 1.0
                next(g, None)
    for g, _ in gens_and_paces:
        for _ in g:
            pass


def _inproj_kernel(x_ref, nw_ref, w_ref, o_ref, xn_ref):
    @pl.when(pl.program_id(1) == 0)
    def _():
        x = x_ref[...]
        ms = jnp.mean(x * x, axis=-1, keepdims=True)
        xn_ref[...] = (x * lax.rsqrt(ms + RMS_EPS) * nw_ref[...]).astype(BF16)

    o_ref[...] = jnp.dot(xn_ref[...], w_ref[...], preferred_element_type=F32).astype(o_ref.dtype)


def _inproj(x2, nw, w, tm, tn, out_dtype):
    t, d = x2.shape
    npad = w.shape[1]
    return pl.pallas_call(
        _inproj_kernel,
        grid=(t // tm, npad // tn),
        in_specs=[
            pl.BlockSpec((tm, d), lambda i, j: (i, 0)),
            pl.BlockSpec((1, d), lambda i, j: (0, 0)),
            pl.BlockSpec((d, tn), lambda i, j: (0, j)),
        ],
        out_specs=pl.BlockSpec((tm, tn), lambda i, j: (i, j)),
        out_shape=jax.ShapeDtypeStruct((t, npad), out_dtype),
        scratch_shapes=[pltpu.VMEM((tm, d), BF16)],
        compiler_params=pltpu.CompilerParams(
            dimension_semantics=("arbitrary", "arbitrary"), vmem_limit_bytes=VMEM_LIMIT,
            allow_input_fusion=[False, False, True]),
        name="inproj",
    )(x2, nw, w)


def _block_diag(x, lane_head):
    return jnp.concatenate(
        [jnp.where(lane_head == h, x, 0.0) for h in range(HEAD_GROUP)], axis=0)


def _rwkv_chunk_pre(chunks, out):
    L = RWKV_CHUNK
    W = GROUP_W
    row = _iota((L, W), 0)
    lane = _iota((L, W), 1)
    lane_head = lane >> 6
    lane_in = lane & (RWKV_HEAD_DIM - 1)
    incl = lane_in <= row
    strict = lane_in < row
    eye = jnp.where(lane_in == row, 1.0, 0.0).astype(F32)
    tri = jnp.where(_iota((L, L), 1) <= _iota((L, L), 0), 1.0, 0.0).astype(BF16)

    bd = functools.partial(_block_diag, lane_head=lane_head)
    n = range(len(chunks))
    r, k, v, ld, av, bv = ([c[i] for c in chunks] for i in range(6))

    cum = [_dot_left01(tri, ld[i], 2) for i in n]
    g_last = [jnp.exp2(cum[i][L - 1:L, :]) for i in n]
    r_t = [r[i] * jnp.exp2(cum[i]) for i in n]
    a_t = [av[i] * jnp.exp2(cum[i] - ld[i]) for i in n]
    g_inv = [jnp.exp2(-cum[i]) for i in n]
    b_t = [bv[i] * g_inv[i] for i in n]
    k_t = [k[i] * g_inv[i] for i in n]
    g_tail = [g_last[i] * g_inv[i] for i in n]
    b_e = [bv[i] * g_tail[i] for i in n]
    k_e = [k[i] * g_tail[i] for i in n]
    yield

    prod = [_dot_nt(jnp.concatenate([a_t[i], r_t[i]], axis=0),
                    jnp.concatenate([bd(b_t[i]), bd(k_t[i])], axis=0)) for i in n]
    m_ab = [jnp.where(strict, prod[i][:L, :W], 0.0) for i in n]
    m_ak = [jnp.where(strict, prod[i][:L, W:], 0.0) for i in n]
    p_rb = [jnp.where(incl, prod[i][L:, :W], 0.0) for i in n]
    p_rk = [jnp.where(incl, prod[i][L:, W:], 0.0) for i in n]
    yield

    x = [_dot(m_ab[i], bd(m_ab[i])) for i in n]
    t_inv = [eye + m_ab[i] for i in n]
    mp_v = [_dot(jnp.concatenate([m_ak[i], p_rk[i]], axis=0), bd(v[i])) for i in n]
    yield
    for _ in range(4):
        q = [_dot(jnp.concatenate([x[i], t_inv[i]], axis=0), bd(x[i])) for i in n]
        x = [q[i][:L] for i in n]
        t_inv = [t_inv[i] + q[i][L:] for i in n]
        yield
    t_inv = [t_inv[i] + _dot(t_inv[i], bd(x[i])) for i in n]
    yield

    au = [_dot(t_inv[i], jnp.concatenate([bd(a_t[i]), bd(mp_v[i][:L])], axis=1)) for i in n]
    yield
    ry = [_dot(p_rb[i], jnp.concatenate([bd(au[i][:, :W]), bd(au[i][:, W:])], axis=1)) for i in n]
    r_hat = [r_t[i] + ry[i][:, :W] for i in n]
    y0 = [ry[i][:, W:] + mp_v[i][L:] for i in n]
    out.extend(dict(ra=jnp.concatenate([r_hat[i], au[i][:, :W]], axis=0), y0=y0[i], u0=au[i][:, W:],
                    v=v[i], bk_e=jnp.concatenate([b_e[i], k_e[i]], axis=0),
                    g_last=g_last[i]) for i in n)


def _rwkv_chunk_state(s, c, same_head):
    L = RWKV_CHUNK
    ra = _dot_nt(c["ra"], s)
    yield None
    y = ra[:L] + c["y0"]
    uv = jnp.concatenate([(ra[L:] + c["u0"]).astype(BF16), c["v"].astype(BF16)], axis=0)
    s = s * c["g_last"] + jnp.where(same_head, _dot_tn(uv, c["bk_e"]), 0.0)
    yield y, s


def _rwkv_kernel(has_vres, n_chunks, nt, *refs):
    it = iter(refs)
    r_ref, k_ref, v_ref, z_ref, lora_ref = (next(it) for _ in range(5))
    vres_ref = next(it) if has_vres else None
    vf_ref = next(it) if has_vres else None
    pv_ref, pvp_ref, mul_ref, w2_ref, a2_ref = (next(it) for _ in range(5))
    v2_ref = next(it) if has_vres else None
    y_ref = next(it)
    vout_ref = None if has_vres else next(it)
    s_ref, cr, ck, cv, cz, cl, cvr = (next(it) for _ in range(7))
    st_ra, st_y0, st_u0, st_v, st_bke, st_gl, st_bon, st_zg = (next(it) for _ in range(8))
    stash = (st_ra, st_y0, st_u0, st_v, st_bke, st_gl, st_bon, st_zg)

    n = pl.program_id(0)
    last = pl.num_programs(0) - 2
    t_cur = jnp.minimum(n, last) % nt
    t_prev = jnp.maximum(n - 1, 0) % nt
    wslot = n % 2
    rslot = 1 - wslot
    advance = n < last

    @pl.when(n == 0)
    def _():
        s_ref[...] = jnp.zeros_like(s_ref)
        for st in stash:
            st[1] = jnp.zeros(st.shape[1:], st.dtype)

    @pl.when(t_cur == 0)
    def _():
        for c in (cr, ck, cv, cz, cl, cvr):
            c[0:SUBLANES, :] = jnp.zeros((SUBLANES, c.shape[1]), c.dtype)

    pv = pv_ref[...]
    prm = lambda i: pv[i:i + 1, :]
    mu_r, mu_k, mu_v, mu_z, w0, a0, k_k, k_a, r_k, _, _, v0 = (prm(i) for i in range(12))
    gn_w, gn_b = pvp_ref[9:10, :], pvp_ref[10:11, :]

    row = _iota((GROUP_W, GROUP_W), 0)
    lane = _iota((GROUP_W, GROUP_W), 1)
    same_head = (row >> 6) == (lane >> 6)
    seg_ones = jnp.where(same_head, 1.0, 0.0).astype(BF16)
    head_sum = lambda x: _dot_right01(x, seg_ones, 1)
    L = RWKV_CHUNK

    tl = r_ref.shape[0]

    def current_block(r0, r1):
        def shifted(ref, ext, mu):
            u = ref[r0:r1, :].astype(F32)
            ext[SUBLANES + r0:SUBLANES + r1, :] = u
            prev = ext[SUBLANES - 1 + r0:SUBLANES - 1 + r1, :]
            if r1 == tl:
                ext[0:SUBLANES, :] = jnp.where(advance, u[r1 - r0 - SUBLANES:, :], ext[0:SUBLANES, :])
            return u + (prev - u) * mu

        c0, nc = r0 // L, (r1 - r0) // L
        r = shifted(r_ref, cr, mu_r)
        k = shifted(k_ref, ck, mu_k)
        yield
        v = shifted(v_ref, cv, mu_v)
        lora = shifted(lora_ref, cl, mul_ref[0:1, :])
        yield
        ld = -(DECAY_SCALE * LOG2_E) * _sigmoid(w0 + _dot(jnp.tanh(lora), w2_ref[...]))
        a = _sigmoid(a0 + _dot(lora, a2_ref[...]))
        yield
        kk = k * k_k
        kk = kk * lax.rsqrt(jnp.maximum(head_sum(kk * kk), 1e-24))
        k = k * (1.0 + (a - 1.0) * k_a)
        yield
        if has_vres:
            vres = shifted(vres_ref, cvr, mul_ref[1:2, :])
            v = v + (vf_ref[r0:r1, :] - v) * _sigmoid(v0 + _dot(vres, v2_ref[...]))
        else:
            vout_ref[r0:r1, :] = v
        yield
        z = shifted(z_ref, cz, mu_z)
        st_zg[wslot, r0:r1, :] = _silu(z)
        st_bon[wslot, r0:r1, :] = head_sum(r * k * r_k) * v
        yield
        av = -kk
        bv = kk * a
        ins = [[x[c * L:(c + 1) * L, :] for x in (r, k, v, ld, av, bv)] for c in range(nc)]
        out = []
        yield from _rwkv_chunk_pre(ins, out)
        for c, o in enumerate(out, start=c0):
            st_ra[wslot, 2 * c * L:2 * (c + 1) * L, :] = o["ra"].astype(BF16)
            st_bke[wslot, 2 * c * L:2 * (c + 1) * L, :] = o["bk_e"].astype(BF16)
            st_y0[wslot, c * L:(c + 1) * L, :] = o["y0"]
            st_u0[wslot, c * L:(c + 1) * L, :] = o["u0"]
            st_v[wslot, c * L:(c + 1) * L, :] = o["v"].astype(BF16)
            st_gl[wslot, c * SUBLANES:(c + 1) * SUBLANES, :] = jnp.broadcast_to(
                o["g_last"], (SUBLANES, GROUP_W))

    def previous_block():
        s = jnp.where(t_prev == 0, 0.0, s_ref[...])
        ys = []
        for c in range(n_chunks):
            cd = dict(ra=st_ra[rslot, 2 * c * L:2 * (c + 1) * L, :],
                      bk_e=st_bke[rslot, 2 * c * L:2 * (c + 1) * L, :],
                      y0=st_y0[rslot, c * L:(c + 1) * L, :], u0=st_u0[rslot, c * L:(c + 1) * L, :],
                      v=st_v[rslot, c * L:(c + 1) * L, :],
                      g_last=st_gl[rslot, c * SUBLANES:c * SUBLANES + 1, :])
            for res in _rwkv_chunk_state(s, cd, same_head):
                yield
            y, s = res
            ys.append(y)
        s_ref[...] = s
        y = jnp.concatenate(ys, axis=0)
        inv_n = 1.0 / RWKV_HEAD_DIM
        mean = head_sum(y) * inv_n
        yc = y - mean
        yield
        var = head_sum(yc * yc) * inv_n
        y = yc * lax.rsqrt(var + RWKV_GN_EPS) * gn_w + gn_b
        y_ref[...] = ((y + st_bon[rslot]) * st_zg[rslot]).astype(y_ref.dtype)

    _interleave(20, (current_block(0, tl), 1.0), (previous_block(), 1.0))


def _rwkv(proj, v_first, pvec, mul, w2p, a2p, v2p, bsz, seq, tl):
    has_vres = v_first is not None
    nt = seq // tl
    t = bsz * seq
    gw = GROUP_W
    hg = RWKV_HEADS // HEAD_GROUP
    n_blocks = bsz * hg * nt
    cur = lambda n: jnp.minimum(n, n_blocks - 1)
    prv = lambda n: jnp.maximum(n - 1, 0)
    rows = lambda m: (m // (hg * nt)) * nt + m % nt
    grp = lambda m: (m // nt) % hg
    colblk = lambda off: (lambda n: (rows(cur(n)), off // gw + grp(cur(n))))
    smallblk = lambda off: (lambda n: (rows(cur(n)), off // LANES))
    in_specs = [pl.BlockSpec((tl, gw), colblk(OFF_R)),
                pl.BlockSpec((tl, gw), colblk(OFF_K)),
                pl.BlockSpec((tl, gw), colblk(OFF_V)),
                pl.BlockSpec((tl, gw), colblk(OFF_Z)),
                pl.BlockSpec((tl, LANES), smallblk(OFF_LORA))]
    args = [proj, proj, proj, proj, proj]
    if has_vres:
        in_specs += [pl.BlockSpec((tl, LANES), smallblk(OFF_VRES)),
                     pl.BlockSpec((tl, gw), colblk(0))]
        args += [proj, v_first]
    in_specs += [pl.BlockSpec((16, gw), lambda n: (0, grp(cur(n)))),
                 pl.BlockSpec((16, gw), lambda n: (0, grp(prv(n)))),
                 pl.BlockSpec((SUBLANES, LANES), lambda n: (0, 0)),
                 pl.BlockSpec((LANES, gw), lambda n: (0, grp(cur(n)))),
                 pl.BlockSpec((LANES, gw), lambda n: (0, grp(cur(n))))]
    args += [pvec, pvec, mul, w2p, a2p]
    if has_vres:
        in_specs += [pl.BlockSpec((LANES, gw), lambda n: (0, grp(cur(n))))]
        args += [v2p]
    out_shape = [jax.ShapeDtypeStruct((t, RWKV_WIDTH), BF16)]
    out_specs = [pl.BlockSpec((tl, gw), lambda n: (rows(prv(n)), grp(prv(n))))]
    if not has_vres:
        out_shape.append(jax.ShapeDtypeStruct((t, RWKV_WIDTH), F32))
        out_specs.append(pl.BlockSpec((tl, gw), colblk(0)))
    n_chunks = tl // RWKV_CHUNK
    stash = lambda rows, dtype: pltpu.VMEM((2, rows, gw), dtype)
    scratch = ([pltpu.VMEM((gw, gw), F32)]
               + [pltpu.VMEM((SUBLANES + tl, gw), F32)] * 4
               + [pltpu.VMEM((SUBLANES + tl, LANES), F32)] * 2
               + [stash(2 * tl, BF16), stash(tl, F32), stash(tl, F32), stash(tl, BF16),
                  stash(2 * tl, BF16), stash(SUBLANES * n_chunks, F32), stash(tl, F32), stash(tl, F32)])
    res = pl.pallas_call(
        functools.partial(_rwkv_kernel, has_vres, n_chunks, nt),
        grid=(n_blocks + 1,),
        in_specs=in_specs, out_specs=out_specs, out_shape=out_shape,
        scratch_shapes=scratch,
        compiler_params=pltpu.CompilerParams(
            dimension_semantics=("arbitrary",), vmem_limit_bytes=VMEM_LIMIT),
        name="rwkv",
    )(*args)
    return (res[0], v_first) if has_vres else (res[0], res[1])


def _ssd_chunk_pre(xs, bm, cm, dt, a_neg):
    L = SSD_CHUNK
    gw = SSM_GROUP_W
    tri = jnp.where(_iota((L, L), 1) <= _iota((L, L), 0), 1.0, 0.0).astype(BF16)
    causal = _iota((L, L), 1) <= _iota((L, L), 0)
    lane_lo = _iota((L, LANES), 1) < SSM_HEAD_DIM
    heads = range(SSM_GROUP_HEADS)
    pairs = range(SSM_GROUP_HEADS // 2)

    def per_head_lanes(cols):
        return jnp.concatenate([jnp.where(lane_lo, cols[2 * p], cols[2 * p + 1]) for p in pairs], axis=1)

    cum = _dot_left01(tri, dt * a_neg, 3)
    cum_b = [jnp.broadcast_to(cum[:, h:h + 1], (L, LANES)) for h in heads]
    expand = jnp.where(_iota((LANES, gw), 0) == (_iota((LANES, gw), 1) >> 6), 1.0, 0.0).astype(BF16)
    dt_e = _dot_right01(dt, expand, 2)
    cum_e = per_head_lanes(cum_b)
    ecum_e = jnp.exp2(cum_e)
    tail_e = jnp.exp2(cum_e[L - 1:L, :] - cum_e)
    xdt = xs * dt_e

    cb = _dot_nt(cm, bm)
    cum_t = cum.T
    pieces = []
    for pair in pairs:
        x_pair = xdt[:, pair * LANES:(pair + 1) * LANES]
        outs = []
        for hh in (2 * pair, 2 * pair + 1):
            seg = cum_b[hh] - cum_t[hh:hh + 1, :]
            w_h = cb * jnp.exp2(jnp.where(causal, seg, -jnp.inf))
            outs.append(_dot(w_h, x_pair))
        pieces.append(jnp.where(lane_lo, outs[0], outs[1]))
    y_intra = jnp.concatenate(pieces, axis=1)

    return y_intra, ecum_e, _dot_tn(bm, xdt * tail_e)


def _ssd_kernel(n_chunks, z_ref, x_ref, b_ref, c_ref, dt_ref, pw_ref, ps_ref, y_ref,
                st_ref, cx, cb_c, cc_c, px, pb_c, pc_c):
    @pl.when(pl.program_id(2) == 0)
    def _():
        st_ref[...] = jnp.zeros_like(st_ref)
        for c in (cx, cb_c, cc_c, px, pb_c, pc_c):
            c[0:SUBLANES, :] = jnp.zeros((SUBLANES, c.shape[1]), c.dtype)

    pw = pw_ref[...]
    ps = ps_ref[...]

    def conv_silu(ref, ext, ext_p, taps, bias):
        assert CONV_WIDTH == 4
        u = ref[...].astype(F32)
        u1, = _delayed_rows(ext, u, (1,))
        p2, = _delayed_rows(ext_p, u * taps[1] + u1 * taps[0], (2,))
        return _silu(u * taps[3] + u1 * taps[2] + p2 + bias)

    xs = conv_silu(x_ref, cx, px, [pw[i:i + 1, :] for i in range(4)], pw[4:5, :])
    bm = conv_silu(b_ref, cb_c, pb_c, [ps[i:i + 1, :] for i in range(4)], ps[4:5, :])
    cm = conv_silu(c_ref, cc_c, pc_c, [ps[i:i + 1, :] for i in range(5, 9)], ps[9:10, :])
    dt = _softplus(dt_ref[...].astype(F32) + ps[10:11, :])
    a_neg = -LOG2_E * jnp.exp(ps[11:12, :])

    L = SSD_CHUNK
    sl = lambda t, c: t[c * L:(c + 1) * L, :]
    pre = [_ssd_chunk_pre(sl(xs, c), sl(bm, c), sl(cm, c), sl(dt, c), a_neg) for c in range(n_chunks)]
    st = st_ref[...]
    ys = []
    for c, (y_intra, ecum_e, upd) in enumerate(pre):
        ys.append(y_intra + _dot(sl(cm, c), st) * ecum_e)
        st = st * ecum_e[L - 1:L, :] + upd
    st_ref[...] = st

    z = z_ref[...].astype(F32)
    y = (jnp.concatenate(ys, axis=0) + pw[5:6, :] * xs) * _silu(z)
    y = y * lax.rsqrt(jnp.mean(y * y, axis=-1, keepdims=True) + RMS_EPS)
    y_ref[...] = (y * pw[6:7, :]).astype(y_ref.dtype)


def _ssd(proj, pwide, psmall, bsz, seq, tl):
    nt = seq // tl
    t = bsz * seq
    gw = SSM_GROUP_W
    blk = lambda off, w: (lambda b, g, i: (b * nt + i, off // w + g))
    return pl.pallas_call(
        functools.partial(_ssd_kernel, tl // SSD_CHUNK),
        grid=(bsz, SSM_GROUPS, nt),
        in_specs=[pl.BlockSpec((tl, gw), blk(OFF_SSM_Z, gw)),
                  pl.BlockSpec((tl, gw), blk(OFF_XS, gw)),
                  pl.BlockSpec((tl, LANES), blk(OFF_B, LANES)),
                  pl.BlockSpec((tl, LANES), blk(OFF_C, LANES)),
                  pl.BlockSpec((tl, LANES), blk(OFF_DT, LANES)),
                  pl.BlockSpec((16, gw), lambda b, g, i: (0, g)),
                  pl.BlockSpec((16, LANES), lambda b, g, i: (0, g))],
        out_specs=pl.BlockSpec((tl, gw), lambda b, g, i: (b * nt + i, g)),
        out_shape=jax.ShapeDtypeStruct((t, SSM_WIDTH), BF16),
        scratch_shapes=[pltpu.VMEM((SSM_STATE, gw), F32),
                        pltpu.VMEM((SUBLANES + tl, gw), F32),
                        pltpu.VMEM((SUBLANES + tl, LANES), F32),
                        pltpu.VMEM((SUBLANES + tl, LANES), F32),
                        pltpu.VMEM((SUBLANES + tl, gw), F32),
                        pltpu.VMEM((SUBLANES + tl, LANES), F32),
                        pltpu.VMEM((SUBLANES + tl, LANES), F32)],
        compiler_params=pltpu.CompilerParams(
            dimension_semantics=("arbitrary", "arbitrary", "arbitrary"),
            vmem_limit_bytes=VMEM_LIMIT),
        name="ssd",
    )(proj, proj, proj, proj, proj, pwide, psmall)


def _merge_kernel(final, yr_ref, ys_ref, gr_ref, gs_ref, x_ref, wr_ref, ws_ref, wo_ref, fw_ref, o_ref):
    pr = jnp.dot(yr_ref[...], wr_ref[...], preferred_element_type=F32)
    ps = jnp.dot(ys_ref[...], ws_ref[...], preferred_element_type=F32)
    h = _sigmoid(gr_ref[...].astype(F32)) * pr + _sigmoid(gs_ref[...].astype(F32)) * ps
    o = x_ref[...] + jnp.dot(h.astype(BF16), wo_ref[...], preferred_element_type=F32)
    if final:
        o = o * lax.rsqrt(jnp.mean(o * o, axis=-1, keepdims=True) + RMS_EPS) * fw_ref[...]
    o_ref[...] = o


def _merge(y_rwkv, y_ssm, proj, x2, wr, ws, wo, fw, final, tm):
    t, d = x2.shape
    const = lambda i: (0, 0)
    return pl.pallas_call(
        functools.partial(_merge_kernel, final),
        grid=(t // tm,),
        in_specs=[pl.BlockSpec((tm, RWKV_WIDTH), lambda i: (i, 0)),
                  pl.BlockSpec((tm, SSM_WIDTH), lambda i: (i, 0)),
                  pl.BlockSpec((tm, d), lambda i: (i, OFF_G_RWKV // d)),
                  pl.BlockSpec((tm, d), lambda i: (i, OFF_G_SSM // d)),
                  pl.BlockSpec((tm, d), lambda i: (i, 0)),
                  pl.BlockSpec((RWKV_WIDTH, d), const),
                  pl.BlockSpec((SSM_WIDTH, d), const),
                  pl.BlockSpec((d, d), const),
                  pl.BlockSpec((1, d), const)],
        out_specs=pl.BlockSpec((tm, d), lambda i: (i, 0)),
        out_shape=jax.ShapeDtypeStruct((t, d), F32),
        compiler_params=pltpu.CompilerParams(
            dimension_semantics=("arbitrary",), vmem_limit_bytes=VMEM_LIMIT,
            allow_input_fusion=[False] * 5 + [True] * 3 + [False]),
        name="merge",
    )(y_rwkv, y_ssm, proj, proj, x2, wr, ws, wo, fw)


def _pad_cols(w, width):
    return jnp.pad(w, ((0, 0), (0, width - w.shape[1])))


def _pad_rows(w, rows):
    return jnp.pad(w, ((0, rows - w.shape[0]), (0, 0)))


def _proj_weight(w_in, w_vres):
    d = w_in.shape[0]
    c = lambda a, b: w_in[:, a:b]
    rw = 0
    r, k, v = c(rw, rw + 1024), c(rw + 1024, rw + 2048), c(rw + 2048, rw + 3072)
    lora = c(rw + 3072, rw + 3200)
    z_rwkv = c(rw + 3200, rw + 4224)
    sm = 4224
    ssm_z = c(sm, sm + 2048)
    xs = c(sm + 2048, sm + 4096)
    bm = c(sm + 4096, sm + 4608)
    cm = c(sm + 4608, sm + 5120)
    dt = c(sm + 5120, sm + 5152)
    gt = sm + 5152
    g_rwkv, g_ssm = c(gt, gt + 1024), c(gt + 1024, gt + 2048)
    vres = jnp.zeros((d, LANES), w_in.dtype) if w_vres is None else _pad_cols(w_vres, LANES)
    dt4 = jnp.pad(dt.reshape(d, SSM_GROUPS, SSM_GROUP_HEADS),
                  ((0, 0), (0, 0), (0, LANES - SSM_GROUP_HEADS))).reshape(d, SSM_GROUPS * LANES)
    w = jnp.concatenate([ssm_z, g_rwkv, g_ssm, r, k, v, z_rwkv, xs, bm, cm, lora, vres, dt4], axis=1)
    return _pad_cols(w, PROJ_W).astype(BF16)


def _group_lanes(vec):
    return jnp.pad(vec.reshape(SSM_GROUPS, SSM_GROUP_HEADS),
                   ((0, 0), (0, LANES - SSM_GROUP_HEADS))).reshape(1, SSM_GROUPS * LANES)


def kernel(x, norm_w, w_in, w_in_vres, mu_rwkv, mu_vres, decay_w0, decay_w2, iclr_a0, iclr_a2, vres_v0, vres_v2, k_k, k_a, r_k, gn_w, gn_b, w_out_rwkv, conv_w, conv_b, dt_bias, a_log, d_skip, ssm_norm_w, w_out_ssm, w_out, final_norm_w):
    bsz, seq, d = x.shape
    depth = norm_w.shape[0]
    t = bsz * seq
    x2 = x.reshape(t, d)
    tm_proj = min(1024, t)
    tn_proj = 3072
    tl_rwkv = min(512, seq)
    tl_ssd = min(1024, seq)
    tm_merge = min(512, t)
    proj_dtype = BF16

    v_first = None
    for i in range(depth):
        w = _proj_weight(w_in[i], None if i == 0 else w_in_vres[i - 1])
        proj = _inproj(x2, norm_w[i].reshape(1, d), w, tm_proj, tn_proj, proj_dtype)

        mu = mu_rwkv[i]
        rows = [mu[0:1024], mu[1024:2048], mu[2048:3072], mu[3200:4224], decay_w0[i], iclr_a0[i],
                k_k[i], k_a[i], r_k[i].reshape(-1), gn_w[i], gn_b[i],
                vres_v0[i - 1] if i > 0 else jnp.zeros((RWKV_WIDTH,), F32)]
        pvec = _pad_rows(jnp.stack(rows, axis=0), 16)
        mu_small = jnp.stack([mu[3072:3200],
                              jnp.pad(mu_vres[i - 1], (0, LANES - VRES_LORA)) if i > 0
                              else jnp.zeros((LANES,), F32)], axis=0)
        mu_small = _pad_rows(mu_small, SUBLANES)
        w2p = _pad_rows(decay_w2[i], LANES).astype(BF16)
        a2p = jnp.concatenate([jnp.zeros_like(iclr_a2[i]), iclr_a2[i]], axis=0).astype(BF16)
        v2p = _pad_rows(vres_v2[i - 1], LANES).astype(BF16) if i > 0 else None
        y_rwkv, v_first = _rwkv(proj, v_first, pvec, mu_small, w2p, a2p, v2p, bsz, seq, tl_rwkv)

        cw, cb = conv_w[i], conv_b[i]
        pwide = jnp.concatenate([cw[:, :SSM_WIDTH], cb[None, :SSM_WIDTH],
                                 jnp.repeat(d_skip[i], SSM_HEAD_DIM)[None, :],
                                 ssm_norm_w[i][None, :]], axis=0)
        pwide = _pad_rows(pwide, 16)
        nb = SSM_GROUPS * SSM_STATE
        psmall = jnp.concatenate([cw[:, SSM_WIDTH:SSM_WIDTH + nb], cb[None, SSM_WIDTH:SSM_WIDTH + nb],
                                  cw[:, SSM_WIDTH + nb:], cb[None, SSM_WIDTH + nb:],
                                  _group_lanes(dt_bias[i]), _group_lanes(a_log[i])], axis=0)
        psmall = _pad_rows(psmall, 16)
        y_ssm = _ssd(proj, pwide, psmall, bsz, seq, tl_ssd)

        final = i == depth - 1
        x2 = _merge(y_rwkv, y_ssm, proj, x2, w_out_rwkv[i].astype(BF16), w_out_ssm[i].astype(BF16),
                    w_out[i].astype(BF16), final_norm_w.reshape(1, d), final, tm_merge)
    return x2.reshape(bsz, seq, d)
```
